```python
import math
import jax, jax.numpy as jnp
from jax import lax
import numpy as np

D_MODEL = 1024
BATCH = 8
SEQ = 4096
DEPTH = 4

HG_HEADS = 8
HG_DK = 128
HG_DV = 128
HG_WIDTH = HG_HEADS * HG_DK
HG_CHUNK = 64
LOG_FLOOR = 1e-30
FOX_HEADS = 16
FOX_DH = 64
FOX_WIDTH = FOX_HEADS * FOX_DH
FOX_QBLOCK = 128
FOX_F_BIAS_INIT = 2.0
MASK_VALUE = -1e30
N_EXPERTS = 32
TOP_K = 4
D_EXPERT = 1024
SWIGLU_LIMIT = 7.0
SWIGLU_ALPHA = 1.702
MOE_BLOCK = 256
LN_EPS = 1e-5
RMS_EPS = 1e-6
DEEPNORM_ALPHA = (2 * DEPTH) ** 0.25
DEEPNORM_BETA = (8 * DEPTH) ** -0.25
IN_SIZES = (HG_WIDTH, HG_WIDTH, HG_WIDTH, HG_WIDTH, FOX_WIDTH, FOX_WIDTH, FOX_WIDTH, FOX_HEADS, D_MODEL, D_MODEL)
P_IN = sum(IN_SIZES)

kernel_name = "hgrn2_fox_gated_moe_deepnorm_adaln"


def layer_norm(x, g, b):
    xf = x.astype(jnp.float32)
    mu = jnp.mean(xf, -1, keepdims=True)
    var = jnp.mean(jnp.square(xf - mu), -1, keepdims=True)
    return ((xf - mu) * lax.rsqrt(var + LN_EPS) * g + b).astype(x.dtype)


def in_projection(h, w):
    outs, off = [], 0
    for n in IN_SIZES:
        outs.append(jnp.einsum('bsd,de->bse', h, w[:, off:off + n]))
        off += n
    return outs


def hgrn2_mixer(q, f_raw, i, g_out, lb, norm_w):
    B, S, _ = q.shape
    f32 = jnp.float32
    n_chunks = S // HG_CHUNK
    lb = lb.astype(f32)
    fr = f_raw.astype(f32)
    sig = jax.nn.sigmoid(fr)
    f = lb + (1.0 - lb) * sig
    log_f = jnp.log(jnp.maximum(f, LOG_FLOOR))
    k = (1.0 - lb) * (1.0 - sig)

    def to_chunks(t, d):
        return t.reshape(B, n_chunks, HG_CHUNK, HG_HEADS, d).transpose(1, 0, 3, 2, 4)

    qc = to_chunks(jax.nn.silu(q.astype(f32)), HG_DK)
    kc = to_chunks(k, HG_DK)
    gc = to_chunks(log_f, HG_DK)
    ic = to_chunks(i.astype(f32), HG_DV)
    causal = jnp.tril(jnp.ones((HG_CHUNK, HG_CHUNK), bool))[:, :, None]

    def step(state, inp):
        qb, kb, gb, ib = inp
        b = jnp.cumsum(gb, axis=2)
        diff = b[:, :, :, None, :] - b[:, :, None, :, :]
        decay = jnp.where(causal, jnp.exp(jnp.where(causal, diff, 0.0)), 0.0)
        scores = jnp.einsum('bhtd,bhsd,bhtsd->bhts', qb, kb, decay)
        o = (jnp.einsum('bhts,bhsv->bhtv', scores, ib)
             + jnp.einsum('bhtd,bhdv->bhtv', qb * jnp.exp(b), state))
        b_last = b[:, :, -1:, :]
        new_state = (state * jnp.exp(b[:, :, -1])[..., None]
                     + jnp.einsum('bhsd,bhsv->bhdv', kb * jnp.exp(b_last - b), ib))
        return new_state, o

    state0 = jnp.zeros((B, HG_HEADS, HG_DK, HG_DV), f32)
    _, oc = lax.scan(step, state0, (qc, kc, gc, ic))
    o = oc.transpose(1, 0, 3, 2, 4).reshape(B, S, HG_HEADS, HG_DV)
    o = o * lax.rsqrt(jnp.mean(o * o, -1, keepdims=True) + RMS_EPS) * norm_w.astype(f32)
    o = o.reshape(B, S, HG_WIDTH) * jax.nn.silu(g_out.astype(f32))
    return o.astype(q.dtype)


def fox_mixer(q, k, v, f_logit):
    B, S, _ = q.shape
    f32 = jnp.float32
    heads = lambda t: t.reshape(B, S, FOX_HEADS, FOX_DH).transpose(0, 2, 1, 3)
    qh, kh, vh = heads(q), heads(k), heads(v)
    F = jnp.cumsum(jax.nn.log_sigmoid(f_logit.astype(f32)), axis=1).transpose(0, 2, 1)
    n_blocks = S // FOX_QBLOCK
    q_blocks = qh.reshape(B, FOX_HEADS, n_blocks, FOX_QBLOCK, FOX_DH).transpose(2, 0, 1, 3, 4)
    F_blocks = F.reshape(B, FOX_HEADS, n_blocks, FOX_QBLOCK).transpose(2, 0, 1, 3)
    starts = jnp.arange(n_blocks, dtype=jnp.int32) * FOX_QBLOCK
    key_pos = jnp.arange(S, dtype=jnp.int32)
    scale = FOX_DH ** -0.5

    def block(args):
        q_blk, F_blk, start = args
        logits = jnp.einsum('bhqd,bhkd->bhqk', q_blk, kh, preferred_element_type=f32) * scale
        logits = logits + (F_blk[..., :, None] - F[..., None, :])
        q_pos = start + jnp.arange(FOX_QBLOCK, dtype=jnp.int32)
        logits = jnp.where(key_pos[None, :] <= q_pos[:, None], logits, MASK_VALUE)
        p = jax.nn.softmax(logits, axis=-1)
        return jnp.einsum('bhqk,bhkd->bhqd', p.astype(vh.dtype), vh)

    o = lax.map(block, (q_blocks, F_blocks, starts))
    return o.transpose(1, 0, 3, 2, 4).reshape(B, S, FOX_WIDTH)


def moe_ffn(h, w_router, b_router, w_gate, b_gate, w_up, b_up, w_down, b_down):
    B, S, D = h.shape
    N = B * S
    f32 = jnp.float32
    xt = h.reshape(N, D)
    logits = (xt @ w_router + b_router).astype(f32)
    top_vals, top_idx = lax.top_k(logits, TOP_K)
    top_w = jax.nn.softmax(top_vals, axis=-1)
    n_assign = N * TOP_K
    e_flat = top_idx.reshape(-1).astype(jnp.int32)
    tok_flat = jnp.arange(n_assign, dtype=jnp.int32) // TOP_K
    w_flat = top_w.reshape(-1)
    order = jnp.argsort(e_flat)
    e_sorted = e_flat[order]
    counts = jnp.bincount(e_flat, length=N_EXPERTS)
    padded = ((counts + MOE_BLOCK - 1) // MOE_BLOCK) * MOE_BLOCK
    start = jnp.cumsum(counts) - counts
    pad_end = jnp.cumsum(padded)
    pad_start = pad_end - padded
    dest = pad_start[e_sorted] + (jnp.arange(n_assign, dtype=jnp.int32) - start[e_sorted])
    cap = ((n_assign + MOE_BLOCK - 1) // MOE_BLOCK) * MOE_BLOCK + N_EXPERTS * MOE_BLOCK
    n_blocks = cap // MOE_BLOCK
    row_tok = jnp.full((cap,), N, jnp.int32).at[dest].set(tok_flat[order])
    row_w = jnp.zeros((cap,), f32).at[dest].set(w_flat[order])
    block_start = jnp.arange(n_blocks, dtype=jnp.int32) * MOE_BLOCK
    block_expert = jnp.minimum(jnp.searchsorted(pad_end, block_start, side='right'), N_EXPERTS - 1)
    x_pad = jnp.concatenate([xt, jnp.zeros((1, D), xt.dtype)], axis=0)
    x_rows = x_pad[row_tok].reshape(n_blocks, MOE_BLOCK, D)

    def expert_block(args):
        xb, e = args
        g = jnp.minimum(xb @ w_gate[e] + b_gate[e], SWIGLU_LIMIT)
        u = jnp.clip(xb @ w_up[e] + b_up[e], -SWIGLU_LIMIT, SWIGLU_LIMIT)
        a = (u + 1.0) * (g * jax.nn.sigmoid(SWIGLU_ALPHA * g))
        return a @ w_down[e] + b_down[e]

    y_rows = lax.map(expert_block, (x_rows, block_expert)).reshape(cap, D)
    y = jax.ops.segment_sum(y_rows * row_w[:, None], row_tok, num_segments=N + 1)[:N]
    return y.reshape(B, S, D).astype(h.dtype)


def setup_inputs(seed: int = 0) -> dict:
    key = jax.random.key(seed)
    ks = jax.random.split(key, 22)
    f32 = jnp.float32
    nrm = lambda k, shape, s: jax.random.normal(k, shape, f32) * s
    beta = DEEPNORM_BETA
    col_scales = (1.0, 1.0, beta, 1.0, 1.0, 1.0, beta, 1.0, 1.0, 1.0)
    col_scale = jnp.concatenate([jnp.full((n,), s, f32) for n, s in zip(IN_SIZES, col_scales)])
    D = D_MODEL
    return {
        "x": nrm(ks[0], (BATCH, SEQ, D), 1.0),
        "c": nrm(ks[1], (BATCH, D), 1.0),
        "w_in": nrm(ks[2], (DEPTH, D, P_IN), D ** -0.5) * col_scale,
        "fox_f_bias": FOX_F_BIAS_INIT + nrm(ks[3], (DEPTH, FOX_HEADS), 0.1),
        "hg_lb_logits": nrm(ks[4], (DEPTH, HG_WIDTH), 0.1),
        "hg_norm_w": 1.0 + nrm(ks[5], (DEPTH, HG_DV), 0.02),
        "w_branch": nrm(ks[6], (DEPTH, HG_WIDTH + FOX_WIDTH, D), HG_WIDTH ** -0.5),
        "w_out": nrm(ks[7], (DEPTH, D, D), D ** -0.5 * beta),
        "ada_w": nrm(ks[8], (DEPTH, D, 6 * D), 0.1 * D ** -0.5),
        "ada_b": nrm(ks[9], (DEPTH, 6 * D), 0.01),
        "ln1_g": 1.0 + nrm(ks[10], (DEPTH, D), 0.02),
        "ln1_b": nrm(ks[11], (DEPTH, D), 0.02),
        "w_router": nrm(ks[12], (DEPTH, D, N_EXPERTS), D ** -0.5),
        "b_router": nrm(ks[13], (DEPTH, N_EXPERTS), 0.01),
        "w_gate": nrm(ks[14], (DEPTH, N_EXPERTS, D, D_EXPERT), D ** -0.5 * beta),
        "b_gate": nrm(ks[15], (DEPTH, N_EXPERTS, D_EXPERT), 0.02),
        "w_up": nrm(ks[16], (DEPTH, N_EXPERTS, D, D_EXPERT), D ** -0.5 * beta),
        "b_up": nrm(ks[17], (DEPTH, N_EXPERTS, D_EXPERT), 0.02),
        "w_down": nrm(ks[18], (DEPTH, N_EXPERTS, D_EXPERT, D), D_EXPERT ** -0.5 * beta),
        "b_down": nrm(ks[19], (DEPTH, N_EXPERTS, D), 0.02),
        "ln2_g": 1.0 + nrm(ks[20], (DEPTH, D), 0.02),
        "ln2_b": nrm(ks[21], (DEPTH, D), 0.02),
    }


def reference(x, c, w_in, fox_f_bias, hg_lb_logits, hg_norm_w, w_branch, w_out, ada_w, ada_b,
              ln1_g, ln1_b, w_router, b_router, w_gate, b_gate, w_up, b_up, w_down, b_down,
              ln2_g, ln2_b):
    p_lb = jax.nn.softmax(hg_lb_logits.astype(jnp.float32), axis=0)
    lb_all = jnp.clip(jnp.cumsum(p_lb, axis=0) - p_lb[0], 0.0, 1.0)
    c_act = jax.nn.silu(c)
    for l in range(DEPTH):
        mod = c_act @ ada_w[l] + ada_b[l]
        shift1, scale1, gate1, shift2, scale2, gate2 = [m[:, None, :] for m in jnp.split(mod, 6, axis=-1)]
        h = x * (1.0 + scale1) + shift1
        hq, hf, hi, hg, aq, ak, av, af, ga, gb = in_projection(h, w_in[l])
        o_a = hgrn2_mixer(hq, hf, hi, hg, lb_all[l], hg_norm_w[l])
        o_b = fox_mixer(aq, ak, av, af + fox_f_bias[l])
        p_a = jnp.einsum('bse,ed->bsd', o_a, w_branch[l, :HG_WIDTH])
        p_b = jnp.einsum('bse,ed->bsd', o_b, w_branch[l, HG_WIDTH:])
        merged = jax.nn.sigmoid(ga) * p_a + jax.nn.sigmoid(gb) * p_b
        y = jnp.einsum('bsd,de->bse', merged, w_out[l])
        x = layer_norm(DEEPNORM_ALPHA * x + (1.0 + gate1) * y, ln1_g[l], ln1_b[l])
        h2 = x * (1.0 + scale2) + shift2
        y2 = moe_ffn(h2, w_router[l], b_router[l], w_gate[l], b_gate[l], w_up[l], b_up[l], w_down[l], b_down[l])
        x = layer_norm(DEEPNORM_ALPHA * x + (1.0 + gate2) * y2, ln2_g[l], ln2_b[l])
    return x
```

```python
import functools
import math

import jax
import jax.numpy as jnp
from jax import lax
from jax.experimental import pallas as pl
from jax.experimental.pallas import tpu as pltpu

F32 = jnp.float32
BF16 = jnp.bfloat16
HIGHEST = lax.Precision.HIGHEST

HG_HEADS = 8
HG_DK = 128
FOX_HEADS = 16
FOX_DH = 64
N_EXPERTS = 32
TOP_K = 4
LOG_FLOOR = 1e-30
MASK_VALUE = -1e30
SWIGLU_LIMIT = 7.0
SWIGLU_ALPHA = 1.702
LN_EPS = 1e-5
RMS_EPS = 1e-6

LANES = 128
BF16_SUBLANES = 16
VMEM_LIMIT = 56 * 1024 * 1024

HG_CHUNK = 64
HG_SUB = 16
HG_TS = 1024
FOX_T = 512
TM = 512
ROUTE_T = 512
CHUNK = BF16_SUBLANES
BLOCK_CHUNKS = 32
NEG_BIG = -1e30


def _cparams(sem):
    return pltpu.CompilerParams(dimension_semantics=sem, vmem_limit_bytes=VMEM_LIMIT)


def _silu(v):
    return v * jax.nn.sigmoid(v)


def _split3(v):
    a = v.astype(BF16)
    r = v - a.astype(F32)
    b = r.astype(BF16)
    c = (r - b.astype(F32)).astype(BF16)
    return a, b, c


def _dot01(m01, v):
    a, b, c = _split3(v)
    d = lambda t: jnp.dot(m01, t, preferred_element_type=F32)
    return d(a) + d(b) + d(c)


def _dot_nt(a, b):
    return lax.dot_general(a, b, (((1,), (1,)), ((), ())), preferred_element_type=F32)


def _dot_tn(a, b):
    return lax.dot_general(a, b, (((0,), (0,)), ((), ())), preferred_element_type=F32)


def _layer_norm(z, g, b):
    mu = jnp.mean(z, axis=-1, keepdims=True)
    zc = z - mu
    var = jnp.mean(zc * zc, axis=-1, keepdims=True)
    return zc * lax.rsqrt(var + LN_EPS) * g + b


def _mod_kernel(c_ref, w_ref, b_ref, o_ref):
    ca = _silu(c_ref[...])
    o_ref[0] = jnp.dot(ca, w_ref[0], precision=HIGHEST, preferred_element_type=F32) + b_ref[0]


def ada_mod(c, ada_w, ada_b):
    depth, d, d6 = ada_w.shape
    b = c.shape[0]
    tn = min(d6, 1536)
    return pl.pallas_call(
        _mod_kernel,
        grid=(depth, d6 // tn),
        in_specs=[pl.BlockSpec((b, d), lambda l, j: (0, 0)),
                  pl.BlockSpec((1, d, tn), lambda l, j: (l, 0, j)),
                  pl.BlockSpec((1, 1, tn), lambda l, j: (l, 0, j))],
        out_specs=pl.BlockSpec((1, b, tn), lambda l, j: (l, 0, j)),
        out_shape=jax.ShapeDtypeStruct((depth, b, d6), F32),
        compiler_params=_cparams(("parallel", "parallel")),
        name="ada_mod",
    )(c, ada_w, ada_b.reshape(depth, 1, d6))


def _in_proj_kernel(x_ref, mod_ref, w_ref, waf_ref, hq_ref, hf_ref, hi_ref, hg_ref,
                    aq_ref, ak_ref, av_ref, af_ref, *, d):
    shift = mod_ref[0, 0:1, :]
    scale = mod_ref[0, 1:2, :]
    h = x_ref[...] * (1.0 + scale) + shift
    hb = h.astype(BF16)

    def proj(g):
        return jnp.dot(hb, w_ref[:, g * d:(g + 1) * d], preferred_element_type=F32)

    hq_ref[...] = proj(0).astype(BF16)
    hf_ref[...] = proj(1)
    hi_ref[...] = proj(2).astype(BF16)
    hg_ref[...] = proj(3).astype(BF16)
    aq_ref[...] = (proj(4) * (FOX_DH ** -0.5)).astype(BF16)
    ak_ref[...] = proj(5).astype(BF16)
    av_ref[...] = proj(6).astype(BF16)
    af_ref[...] = jnp.dot(h, waf_ref[...], precision=HIGHEST, preferred_element_type=F32)


def in_proj(x2, mod, w7, waf, seq):
    n, d = x2.shape
    tpb = seq // TM
    row = lambda i: (i, 0)
    big = lambda dt: jax.ShapeDtypeStruct((n, d), dt)
    return pl.pallas_call(
        functools.partial(_in_proj_kernel, d=d),
        grid=(n // TM,),
        in_specs=[pl.BlockSpec((TM, d), row),
                  pl.BlockSpec((1, 6, d), lambda i: (i // tpb, 0, 0)),
                  pl.BlockSpec((d, 7 * d), lambda i: (0, 0), pipeline_mode=pl.Buffered(1)),
                  pl.BlockSpec((d, LANES), lambda i: (0, 0), pipeline_mode=pl.Buffered(1))],
        out_specs=[pl.BlockSpec((TM, d), row)] * 7 + [pl.BlockSpec((TM, LANES), row)],
        out_shape=[big(BF16), big(F32), big(BF16), big(BF16), big(BF16), big(BF16), big(BF16),
                   jax.ShapeDtypeStruct((n, LANES), F32)],
        compiler_params=_cparams(("parallel",)),
        name="in_proj",
    )(x2, mod, w7, waf)


def _fox_decay_kernel(af_ref, bias_ref, o_ref, *, seq):
    heads = af_ref.shape[1]
    r = lax.broadcasted_iota(jnp.int32, (LANES, LANES), 0)
    c = lax.broadcasted_iota(jnp.int32, (LANES, LANES), 1)
    upper = jnp.where(r <= c, 1.0, 0.0).astype(BF16)
    carry = jnp.zeros((heads, 1), F32)
    for j in range(seq // LANES):
        z = af_ref[0, :, j * LANES:(j + 1) * LANES] + bias_ref[...]
        ls = jnp.minimum(z, 0.0) - jnp.log(1.0 + jnp.exp(-jnp.abs(z)))
        cs = _dot01_right(ls, upper) + carry
        o_ref[0, :, 0, j * LANES:(j + 1) * LANES] = -cs
        carry = cs[:, LANES - 1:LANES]


def _dot01_right(v, m01):
    a, b, c = _split3(v)
    d = lambda t: jnp.dot(t, m01, preferred_element_type=F32)
    return d(a) + d(b) + d(c)


def fox_decay(af_t, bias):
    b, heads, seq = af_t.shape
    return pl.pallas_call(
        functools.partial(_fox_decay_kernel, seq=seq),
        grid=(b,),
        in_specs=[pl.BlockSpec((1, heads, seq), lambda i: (i, 0, 0)),
                  pl.BlockSpec((heads, 1), lambda i: (0, 0))],
        out_specs=pl.BlockSpec((1, heads, 1, seq), lambda i: (i, 0, 0, 0)),
        out_shape=jax.ShapeDtypeStruct((b, heads, 1, seq), F32),
        compiler_params=_cparams(("parallel",)),
        name="fox_decay",
    )(af_t, bias.reshape(heads, 1))


def _hgrn_kernel(q_ref, f_ref, i_ref, g_ref, lbl_ref, nw_ref, o_ref, st_ref, *, layer, ts):
    C, SUB = HG_CHUNK, HG_SUB
    nsub = C // SUB

    @pl.when(pl.program_id(2) == 0)
    def _():
        st_ref[...] = jnp.zeros_like(st_ref)

    lg = lbl_ref[...]
    depth = lg.shape[0]
    rows = [lg[i:i + 1, :] for i in range(depth)]
    mx = functools.reduce(jnp.maximum, rows)
    ex = [jnp.exp(rw - mx) for rw in rows]
    den = functools.reduce(lambda a, b: a + b, ex)
    lb = jnp.zeros_like(mx)
    for i in range(1, layer + 1):
        lb = lb + ex[i] / den
    lb = jnp.clip(lb, 0.0, 1.0)
    one_m_lb = 1.0 - lb
    nw = nw_ref[...]

    rr = lax.broadcasted_iota(jnp.int32, (C, C), 0)
    cc = lax.broadcasted_iota(jnp.int32, (C, C), 1)
    tril = jnp.where(cc <= rr, 1.0, 0.0).astype(BF16)
    sub8 = lax.broadcasted_iota(jnp.int32, (8, LANES), 0)
    lane8 = lax.broadcasted_iota(jnp.int32, (8, LANES), 1)
    zeros_sub = jnp.zeros((SUB, LANES), F32)

    def chunk(ci, carry):
        r0 = pl.multiple_of(ci * C, C)
        q = q_ref[pl.ds(r0, C), :].astype(F32)
        fr = f_ref[pl.ds(r0, C), :]
        iv = i_ref[pl.ds(r0, C), :]
        go = g_ref[pl.ds(r0, C), :].astype(F32)
        sig = jax.nn.sigmoid(fr)
        f = lb + one_m_lb * sig
        g = jnp.log(jnp.maximum(f, LOG_FLOOR))
        k = one_m_lb * (1.0 - sig)
        qs = _silu(q)
        b = _dot01(tril, g)
        r_end = [b[(j + 1) * SUB - 1:(j + 1) * SUB, :] for j in range(nsub)]
        bc = lambda v: jnp.broadcast_to(v, (SUB, LANES))
        r_prev = jnp.concatenate([zeros_sub] + [bc(r_end[j]) for j in range(nsub - 1)], axis=0)
        r_own = jnp.concatenate([bc(r_end[j]) for j in range(nsub)], axis=0)
        r_last = r_end[nsub - 1]
        qhat = qs * jnp.exp(b - r_prev)
        khat = k * jnp.exp(r_own - b)
        qtil = qhat * jnp.exp(r_prev)
        kst = khat * jnp.exp(r_last - r_own)

        lhs, rhs = [], []
        for j in range(nsub - 1):
            lo = (j + 1) * SUB
            qj = qs[lo:, :] * jnp.exp(b[lo:, :] - r_end[j])
            lhs.append(jnp.concatenate([jnp.zeros((lo, LANES), F32), qj], axis=0))
            parts = []
            if j > 0:
                parts.append(jnp.zeros((j * SUB, LANES), F32))
            parts.append(khat[j * SUB:(j + 1) * SUB, :])
            parts.append(jnp.zeros((C - (j + 1) * SUB, LANES), F32))
            rhs.append(jnp.concatenate(parts, axis=0))
        a_off = _dot_nt(jnp.concatenate(lhs, axis=1).astype(BF16),
                        jnp.concatenate(rhs, axis=1).astype(BF16))

        pieces = []
        for blk in range(nsub):
            base = blk * SUB
            bt = [b[base:base + 8, :], b[base + 8:base + 16, :]]
            qt = [qs[base:base + 8, :], qs[base + 8:base + 16, :]]
            acc = [jnp.zeros((8, LANES), F32), jnp.zeros((8, LANES), F32)]
            for s in range(SUB):
                bs = b[base + s:base + s + 1, :]
                ks = k[base + s:base + s + 1, :]
                for half in range(2):
                    if s >= 8 and half == 0:
                        continue
                    d = bt[half] - bs
                    s_loc = s - 8 * half
                    if s_loc >= 0:
                        d = jnp.where(sub8 >= s_loc, d, NEG_BIG)
                    xval = jnp.exp(d) * (qt[half] * ks)
                    col = jnp.sum(xval, axis=-1, keepdims=True)
                    acc[half] = jnp.where(lane8 == base + s, col, acc[half])
            pieces += acc
        a_diag = jnp.concatenate(pieces, axis=0)
        a = (a_off + a_diag[:, :C]).astype(BF16)

        st = st_ref[...]
        o = jnp.dot(a, iv, preferred_element_type=F32) + _dot_nt(qtil.astype(BF16), st.astype(BF16))
        st_ref[...] = st * jnp.exp(r_last) + _dot_tn(iv, kst.astype(BF16))

        ms = jnp.mean(o * o, axis=-1, keepdims=True)
        o = o * lax.rsqrt(ms + RMS_EPS) * nw
        o_ref[pl.ds(r0, C), :] = (o * _silu(go)).astype(BF16)
        return carry

    lax.fori_loop(0, ts // C, chunk, 0)


def hgrn2(hq, hf, hi, hg, lb_logits, norm_w, *, layer, batch, seq):
    n, width = hq.shape
    heads = width // HG_DK
    ts = min(HG_TS, seq)
    spb = seq // ts
    depth = lb_logits.shape[0]
    blk = pl.BlockSpec((ts, HG_DK), lambda b, h, t: (b * spb + t, h))
    return pl.pallas_call(
        functools.partial(_hgrn_kernel, layer=layer, ts=ts),
        grid=(batch, heads, spb),
        in_specs=[blk, blk, blk, blk,
                  pl.BlockSpec((depth, HG_DK), lambda b, h, t: (0, h)),
                  pl.BlockSpec((1, HG_DK), lambda b, h, t: (0, 0))],
        out_specs=blk,
        out_shape=jax.ShapeDtypeStruct((n, width), BF16),
        scratch_shapes=[pltpu.VMEM((HG_DK, HG_DK), F32)],
        compiler_params=_cparams(("parallel", "parallel", "arbitrary")),
        name="hgrn2",
    )(hq, hf, hi, hg, lb_logits, norm_w.reshape(1, HG_DK))


def _fox_kernel(q_ref, k_ref, v_ref, nf_ref, o_ref, acc_ref, m_ref, l_ref, *, t):
    qi = pl.program_id(2)
    lane = lax.broadcasted_iota(jnp.int32, (t, LANES), 1)
    q = q_ref[...]
    zero = jnp.zeros_like(q)
    qh = [jnp.where(lane < FOX_DH, q, zero), jnp.where(lane >= FOX_DH, q, zero)]
    for h in range(2):
        acc_ref[h] = jnp.zeros((t, LANES), F32)
        m_ref[h] = jnp.full((t, 1), -jnp.inf, F32)
        l_ref[h] = jnp.zeros((t, 1), F32)
    row = lax.broadcasted_iota(jnp.int32, (t, t), 0)
    col = lax.broadcasted_iota(jnp.int32, (t, t), 1)

    def step(j, masked):
        c0 = pl.multiple_of(j * t, t)
        kb = k_ref[pl.ds(c0, t), :]
        vb = v_ref[pl.ds(c0, t), :]
        for h in range(2):
            z = _dot_nt(qh[h], kb) + nf_ref[0, h, :, pl.ds(c0, t)]
            if masked:
                z = jnp.where(col <= row, z, MASK_VALUE)
            m_old = m_ref[h]
            m_new = jnp.maximum(m_old, jnp.max(z, axis=-1, keepdims=True))
            alpha = jnp.exp(m_old - m_new)
            p = jnp.exp(z - m_new)
            l_ref[h] = alpha * l_ref[h] + jnp.sum(p, axis=-1, keepdims=True)
            acc_ref[h] = alpha * acc_ref[h] + jnp.dot(p.astype(BF16), vb, preferred_element_type=F32)
            m_ref[h] = m_new

    def body(j, carry):
        step(j, False)
        return carry

    lax.fori_loop(0, qi, body, 0)
    step(qi, True)
    o0 = acc_ref[0] / l_ref[0]
    o1 = acc_ref[1] / l_ref[1]
    o_ref[...] = jnp.where(lane < FOX_DH, o0, o1).astype(BF16)


def fox_attn(aq, ak, av, negf, *, batch, seq):
    n, width = aq.shape
    pairs = width // LANES
    t = min(FOX_T, seq)
    nq = seq // t
    return pl.pallas_call(
        functools.partial(_fox_kernel, t=t),
        grid=(batch, pairs, nq),
        in_specs=[pl.BlockSpec((t, LANES), lambda b, p, i: (b * nq + i, p)),
                  pl.BlockSpec((seq, LANES), lambda b, p, i: (b, p)),
                  pl.BlockSpec((seq, LANES), lambda b, p, i: (b, p)),
                  pl.BlockSpec((1, 2, 1, seq), lambda b, p, i: (b, p, 0, 0))],
        out_specs=pl.BlockSpec((t, LANES), lambda b, p, i: (b * nq + i, p)),
        out_shape=jax.ShapeDtypeStruct((n, width), BF16),
        scratch_shapes=[pltpu.VMEM((2, t, LANES), F32), pltpu.VMEM((2, t, 1), F32),
                        pltpu.VMEM((2, t, 1), F32)],
        compiler_params=_cparams(("parallel", "parallel", "arbitrary")),
        name="fox_attn",
    )(aq, ak, av, negf)


def _mixer_out_kernel(oa_ref, ob_ref, x_ref, mod_ref, wg_ref, wbr_ref, wo_ref, ln_ref, wr_ref, br_ref,
                      x1_ref, h2_ref, lg_ref, *, d, alpha):
    shift1, scale1, gate1 = mod_ref[0, 0:1, :], mod_ref[0, 1:2, :], mod_ref[0, 2:3, :]
    shift2, scale2 = mod_ref[0, 3:4, :], mod_ref[0, 4:5, :]
    x = x_ref[...]
    hb = (x * (1.0 + scale1) + shift1).astype(BF16)
    ga = jnp.dot(hb, wg_ref[:, :d], preferred_element_type=F32)
    gb = jnp.dot(hb, wg_ref[:, d:], preferred_element_type=F32)
    pa = jnp.dot(oa_ref[...], wbr_ref[:d, :], preferred_element_type=F32)
    pb = jnp.dot(ob_ref[...], wbr_ref[d:, :], preferred_element_type=F32)
    merged = jax.nn.sigmoid(ga) * pa + jax.nn.sigmoid(gb) * pb
    y = jnp.dot(merged.astype(BF16), wo_ref[...], preferred_element_type=F32)
    x1 = _layer_norm(alpha * x + (1.0 + gate1) * y, ln_ref[0:1, :], ln_ref[1:2, :])
    x1_ref[...] = x1
    h2 = x1 * (1.0 + scale2) + shift2
    h2_ref[...] = h2.astype(BF16)
    lg_ref[...] = jnp.dot(h2, wr_ref[...], precision=HIGHEST, preferred_element_type=F32) + br_ref[...]


def mixer_out(oa, ob, x2, mod, wg, wbr, wo, ln, wr, br, *, seq, alpha):
    n, d = x2.shape
    tpb = seq // TM
    row = lambda i: (i, 0)
    const = lambda shape: pl.BlockSpec(shape, lambda i: (0, 0), pipeline_mode=pl.Buffered(1))
    return pl.pallas_call(
        functools.partial(_mixer_out_kernel, d=d, alpha=alpha),
        grid=(n // TM,),
        in_specs=[pl.BlockSpec((TM, d), row), pl.BlockSpec((TM, d), row), pl.BlockSpec((TM, d), row),
                  pl.BlockSpec((1, 6, d), lambda i: (i // tpb, 0, 0)),
                  const((d, 2 * d)), const((2 * d, d)), const((d, d)), const((2, d)),
                  const((d, LANES)), const((1, LANES))],
        out_specs=[pl.BlockSpec((TM, d), row), pl.BlockSpec((TM, d), row), pl.BlockSpec((TM, LANES), row)],
        out_shape=[jax.ShapeDtypeStruct((n, d), F32), jax.ShapeDtypeStruct((n, d), BF16),
                   jax.ShapeDtypeStruct((n, LANES), F32)],
        compiler_params=_cparams(("parallel",)),
        name="mixer_out",
    )(oa, ob, x2, mod, wg, wbr, wo, ln, wr, br)


def _tile_rows(t):
    rows = t * TOP_K + N_EXPERTS * (CHUNK - 1)
    return -(-rows // (8 * CHUNK)) * (8 * CHUNK)


def _dispatch_kernel(lg_ref, h2_ref, xs_ref, rw_ref, cnt_ref, *, t, rows):
    lane = lax.broadcasted_iota(jnp.int32, (t, LANES), 1).astype(F32)
    lg = lg_ref[...]
    onehots, vals = [], []
    for _ in range(TOP_K):
        mx = jnp.max(lg, axis=-1, keepdims=True)
        idx = jnp.min(jnp.where(lg == mx, lane, float(LANES)), axis=-1, keepdims=True)
        oh = lane == idx
        onehots.append(oh)
        vals.append(mx)
        lg = jnp.where(oh, -jnp.inf, lg)
    es = [jnp.exp(v - vals[0]) for v in vals]
    den = functools.reduce(lambda a, b: a + b, es)
    ws = [e / den for e in es]

    member = functools.reduce(lambda a, b: a + b, [jnp.where(oh, 1.0, 0.0) for oh in onehots])
    rr = lax.broadcasted_iota(jnp.int32, (t, t), 0)
    cc = lax.broadcasted_iota(jnp.int32, (t, t), 1)
    strict_lower = jnp.where(cc < rr, 1.0, 0.0).astype(BF16)
    rank = jnp.dot(strict_lower, member.astype(BF16), preferred_element_type=F32)
    cnt = jnp.sum(member, axis=0, keepdims=True)
    nch = jnp.floor((cnt + (CHUNK - 1)) * (1.0 / CHUNK))
    ur = lax.broadcasted_iota(jnp.int32, (LANES, LANES), 0)
    uc = lax.broadcasted_iota(jnp.int32, (LANES, LANES), 1)
    strict_upper = jnp.where(ur < uc, 1.0, 0.0).astype(BF16)
    toff = jnp.dot(jnp.broadcast_to(nch, (8, LANES)).astype(BF16), strict_upper,
                   preferred_element_type=F32)[0:1, :]
    pos = toff * float(CHUNK) + rank

    rw = jnp.full((t, LANES), -1.0, F32)
    for kk in range(TOP_K):
        r_k = jnp.sum(jnp.where(onehots[kk], pos, 0.0), axis=-1, keepdims=True)
        rw = jnp.where(lane == kk, r_k, rw)
        rw = jnp.where(lane == TOP_K + kk, ws[kk], rw)
    rw_ref[...] = rw
    cnt_ref[0] = jnp.broadcast_to(cnt, (8, LANES))

    rt = rw.T
    riota = lax.broadcasted_iota(jnp.int32, (rows, t), 0).astype(F32)
    pm = jnp.zeros((rows, t), F32)
    for kk in range(TOP_K):
        pm = jnp.where(riota == rt[kk:kk + 1, :], 1.0, pm)
    xs_ref[...] = jnp.dot(pm.astype(BF16), h2_ref[...], preferred_element_type=F32).astype(BF16)


def dispatch(logits, h2):
    n, d = h2.shape
    t = min(ROUTE_T, n)
    rows = _tile_rows(t)
    nt = n // t
    return pl.pallas_call(
        functools.partial(_dispatch_kernel, t=t, rows=rows),
        grid=(nt,),
        in_specs=[pl.BlockSpec((t, LANES), lambda i: (i, 0)), pl.BlockSpec((t, d), lambda i: (i, 0))],
        out_specs=[pl.BlockSpec((rows, d), lambda i: (i, 0)), pl.BlockSpec((t, LANES), lambda i: (i, 0)),
                   pl.BlockSpec((1, 8, LANES), lambda i: (i, 0, 0))],
        out_shape=[jax.ShapeDtypeStruct((nt * rows, d), BF16), jax.ShapeDtypeStruct((n, LANES), F32),
                   jax.ShapeDtypeStruct((nt, 8, LANES), F32)],
        compiler_params=_cparams(("parallel",)),
        name="dispatch",
    )(logits, h2)


def _num_blocks(nt, t):
    max_chunks = nt * ((t * TOP_K + N_EXPERTS * (CHUNK - 1)) // CHUNK)
    return -(-max_chunks // BLOCK_CHUNKS) + N_EXPERTS


def _routing_tables(cnt, t):
    nt = cnt.shape[0]
    e = N_EXPERTS
    cpt = _tile_rows(t) // CHUNK
    nb = _num_blocks(nt, t)
    nch = (cnt + (CHUNK - 1)) // CHUNK
    toff = jnp.cumsum(nch, axis=1) - nch
    n_e = jnp.sum(nch, axis=0)
    nblk = (n_e + (BLOCK_CHUNKS - 1)) // BLOCK_CHUNKS
    bend = jnp.cumsum(nblk)
    bstart = bend - nblk
    total_blocks = bend[-1]
    cumj_incl = jnp.cumsum(nch, axis=0)
    cumj = cumj_incl - nch

    g = jnp.arange(nb * BLOCK_CHUNKS, dtype=jnp.int32)
    blk = g // BLOCK_CHUNKS
    ex = jnp.minimum(jnp.sum(bend[None, :] <= blk[:, None], axis=1), e - 1).astype(jnp.int32)
    s = g - bstart[ex] * BLOCK_CHUNKS
    valid = (blk < total_blocks) & (s < n_e[ex])
    cum_e = cumj_incl.T[ex]
    j = jnp.minimum(jnp.sum(cum_e <= s[:, None], axis=1), nt - 1).astype(jnp.int32)
    c = s - cumj[j, ex]
    chunk = j * cpt + toff[j, ex] + c
    src = jnp.where(valid, chunk, 0).astype(jnp.int32)
    dst = jnp.where(valid, chunk, -1).astype(jnp.int32)
    b = jnp.arange(nb, dtype=jnp.int32)
    last = jnp.maximum(total_blocks - 1, 0)
    bex = jnp.minimum(jnp.sum(bend[None, :] <= jnp.minimum(b, last)[:, None], axis=1), e - 1).astype(jnp.int32)
    bvalid = (b < total_blocks).astype(jnp.int32)
    return bex, bvalid, src, dst, nb


def _expert_kernel(bex_ref, bvalid_ref, src_ref, dst_ref, *refs, nb):
    xin = refs[:BLOCK_CHUNKS]
    wg_ref, bg_ref, wu_ref, bu_ref, wd_ref, bd_ref, ys_in_ref, ys_ref, xbuf, ybuf, wbf, sem = refs[BLOCK_CHUNKS:]
    del ys_in_ref
    i = pl.program_id(0)
    slot = i % 2

    def out_copy(step, sl, c):
        row = pl.multiple_of(dst_ref[step * BLOCK_CHUNKS + c] * CHUNK, CHUNK)
        return pltpu.make_async_copy(ybuf.at[sl, pl.ds(c * CHUNK, CHUNK), :],
                                     ys_ref.at[pl.ds(row, CHUNK), :], sem.at[sl])

    def wait_step(step, sl):
        for c in range(BLOCK_CHUNKS):
            @pl.when(dst_ref[step * BLOCK_CHUNKS + c] >= 0)
            def _():
                out_copy(step, sl, c).wait()

    @pl.when(i >= 2)
    def _():
        wait_step(i - 2, slot)

    prev = jnp.maximum(i - 1, 0)
    new_expert = (i == 0) | (bex_ref[i] != bex_ref[prev])

    @pl.when((bvalid_ref[i] == 1) & new_expert)
    def _():
        wbf[0] = wg_ref[0].astype(BF16)
        wbf[1] = wu_ref[0].astype(BF16)
        wbf[2] = wd_ref[0].astype(BF16)

    @pl.when(bvalid_ref[i] == 1)
    def _():
        for c in range(BLOCK_CHUNKS):
            xbuf[c * CHUNK:(c + 1) * CHUNK, :] = xin[c][...]
        xb = xbuf[...]
        g = jnp.minimum(jnp.dot(xb, wbf[0], preferred_element_type=F32) + bg_ref[0], SWIGLU_LIMIT)
        u = jnp.clip(jnp.dot(xb, wbf[1], preferred_element_type=F32) + bu_ref[0], -SWIGLU_LIMIT, SWIGLU_LIMIT)
        a = (u + 1.0) * (g * jax.nn.sigmoid(SWIGLU_ALPHA * g))
        y = jnp.dot(a.astype(BF16), wbf[2], preferred_element_type=F32) + bd_ref[0]
        ybuf[slot] = y.astype(BF16)
        for c in range(BLOCK_CHUNKS):
            @pl.when(dst_ref[i * BLOCK_CHUNKS + c] >= 0)
            def _():
                out_copy(i, slot, c).start()

    @pl.when(i == nb - 1)
    def _():
        @pl.when(i >= 1)
        def _():
            wait_step(i - 1, 1 - slot)
        wait_step(i, slot)


def experts(xs, ys_init, bex, bvalid, src, dst, w_gate, b_gate, w_up, b_up, w_down, b_down, *, nb):
    d = xs.shape[1]
    e, _, de = w_gate.shape
    rows = BLOCK_CHUNKS * CHUNK

    def chunk_spec(c):
        return pl.BlockSpec((CHUNK, d), lambda i, be, bv, sc, dc: (sc[i * BLOCK_CHUNKS + c], 0))

    wspec = lambda shape: pl.BlockSpec(shape, lambda i, be, bv, sc, dc: (be[i], 0, 0))
    grid_spec = pltpu.PrefetchScalarGridSpec(
        num_scalar_prefetch=4,
        grid=(nb,),
        in_specs=[chunk_spec(c) for c in range(BLOCK_CHUNKS)] + [
            wspec((1, d, de)), wspec((1, 1, de)), wspec((1, d, de)), wspec((1, 1, de)),
            wspec((1, de, d)), wspec((1, 1, d)), pl.BlockSpec(memory_space=pl.ANY)],
        out_specs=pl.BlockSpec(memory_space=pl.ANY),
        scratch_shapes=[pltpu.VMEM((rows, d), BF16), pltpu.VMEM((2, rows, d), BF16),
                        pltpu.VMEM((3, d, de), BF16), pltpu.SemaphoreType.DMA((2,))],
    )
    n_in = 4 + BLOCK_CHUNKS + 6
    return pl.pallas_call(
        functools.partial(_expert_kernel, nb=nb),
        grid_spec=grid_spec,
        out_shape=jax.ShapeDtypeStruct(ys_init.shape, BF16),
        input_output_aliases={n_in: 0},
        compiler_params=_cparams(("arbitrary",)),
        name="experts",
    )(bex, bvalid, src, dst, *([xs] * BLOCK_CHUNKS),
      w_gate, b_gate.reshape(e, 1, de), w_up, b_up.reshape(e, 1, de), w_down, b_down.reshape(e, 1, d),
      ys_init)


def _combine_kernel(ys_ref, rw_ref, x1_ref, mod_ref, ln_ref, o_ref, *, t, rows, alpha):
    gate2 = mod_ref[0, 5:6, :]
    rw = rw_ref[...]
    liota = lax.broadcasted_iota(jnp.int32, (t, rows), 1).astype(F32)
    wm = jnp.zeros((t, rows), F32)
    for kk in range(TOP_K):
        wm = jnp.where(liota == rw[:, kk:kk + 1], rw[:, TOP_K + kk:TOP_K + kk + 1], wm)
    y2 = jnp.dot(wm.astype(BF16), ys_ref[...], preferred_element_type=F32)
    o_ref[...] = _layer_norm(alpha * x1_ref[...] + (1.0 + gate2) * y2, ln_ref[0:1, :], ln_ref[1:2, :])


def combine(ys, rw, x1, mod, ln, *, seq, alpha):
    n, d = x1.shape
    t = min(ROUTE_T, n)
    rows = _tile_rows(t)
    tpb = max(seq // t, 1)
    return pl.pallas_call(
        functools.partial(_combine_kernel, t=t, rows=rows, alpha=alpha),
        grid=(n // t,),
        in_specs=[pl.BlockSpec((rows, d), lambda i: (i, 0)),
                  pl.BlockSpec((t, LANES), lambda i: (i, 0)),
                  pl.BlockSpec((t, d), lambda i: (i, 0)),
                  pl.BlockSpec((1, 6, d), lambda i: (i // tpb, 0, 0)),
                  pl.BlockSpec((2, d), lambda i: (0, 0))],
        out_specs=pl.BlockSpec((t, d), lambda i: (i, 0)),
        out_shape=jax.ShapeDtypeStruct((n, d), F32),
        compiler_params=_cparams(("parallel",)),
        name="combine",
    )(ys, rw, x1, mod, ln)


def moe_grouped(logits, h2, w_gate, b_gate, w_up, b_up, w_down, b_down):
    n, d = h2.shape
    t = min(ROUTE_T, n)
    xs, rw, cnt = dispatch(logits, h2)
    cnt_i = cnt[:, 0, :N_EXPERTS].astype(jnp.int32)
    bex, bvalid, src, dst, nb = _routing_tables(cnt_i, t)
    ys_init = jnp.zeros(xs.shape, BF16)
    ys = experts(xs, ys_init, bex, bvalid, src, dst, w_gate, b_gate, w_up, b_up, w_down, b_down, nb=nb)
    return ys, rw


def kernel(x, c, w_in, fox_f_bias, hg_lb_logits, hg_norm_w, w_branch, w_out, ada_w, ada_b,
           ln1_g, ln1_b, w_router, b_router, w_gate, b_gate, w_up, b_up, w_down, b_down,
           ln2_g, ln2_b):
    batch, seq, d = x.shape
    depth = w_in.shape[0]
    n = batch * seq
    alpha = (2 * depth) ** 0.25
    hgw = HG_HEADS * HG_DK
    foxw = FOX_HEADS * FOX_DH
    off_af = 4 * hgw + 3 * foxw
    off_g = off_af + FOX_HEADS

    mod_all = ada_mod(c, ada_w, ada_b).reshape(depth, batch, 6, d)
    x2 = x.reshape(n, d)

    for l in range(depth):
        mod = mod_all[l]
        w7 = w_in[l, :, :off_af].astype(BF16)
        waf = jnp.pad(w_in[l, :, off_af:off_g], ((0, 0), (0, LANES - FOX_HEADS)))
        wg = w_in[l, :, off_g:].astype(BF16)
        hq, hf, hi, hg, aq, ak, av, af = in_proj(x2, mod, w7, waf, seq)

        af_t = af[:, :FOX_HEADS].reshape(batch, seq, FOX_HEADS).transpose(0, 2, 1)
        negf = fox_decay(af_t, fox_f_bias[l])
        o_a = hgrn2(hq, hf, hi, hg, hg_lb_logits, hg_norm_w[l], layer=l, batch=batch, seq=seq)
        o_b = fox_attn(aq, ak, av, negf, batch=batch, seq=seq)

        wr = jnp.pad(w_router[l], ((0, 0), (0, LANES - N_EXPERTS)))
        br = jnp.pad(b_router[l], (0, LANES - N_EXPERTS), constant_values=NEG_BIG).reshape(1, LANES)
        x1, h2, logits = mixer_out(
            o_a, o_b, x2, mod, wg, w_branch[l].astype(BF16), w_out[l].astype(BF16),
            jnp.stack([ln1_g[l], ln1_b[l]]), wr, br, seq=seq, alpha=alpha)

        ys, rw = moe_grouped(logits, h2, w_gate[l], b_gate[l], w_up[l], b_up[l], w_down[l], b_down[l])
        x2 = combine(ys, rw, x1, mod, jnp.stack([ln2_g[l], ln2_b[l]]), seq=seq, alpha=alpha)
    return x2.reshape(batch, seq, d)
```

```python
import functools
import math

import jax
import jax.numpy as jnp
from jax import lax
from jax.experimental import pallas as pl
from jax.experimental.pallas import tpu as pltpu

F32 = jnp.float32
BF16 = jnp.bfloat16
HIGHEST = lax.Precision.HIGHEST

HG_HEADS = 8
HG_DK = 128
FOX_HEADS = 16
FOX_DH = 64
N_EXPERTS = 32
TOP_K = 4
LOG_FLOOR = 1e-30
MASK_VALUE = -1e30
SWIGLU_LIMIT = 7.0
SWIGLU_ALPHA = 1.702
LN_EPS = 1e-5
RMS_EPS = 1e-6
LOG2E = math.log2(math.e)

LANES = 128
BF16_SUBLANES = 16
VMEM_LIMIT = 56 * 1024 * 1024

HG_CHUNK = 64
HG_SUB = 16
HG_TS = 1024
HG_HEADS_PER_STEP = 8
FOX_T = 512
TM = 512
ROUTE_T = 512
CHUNK = BF16_SUBLANES
BLOCK_CHUNKS = 32
NEG_BIG = -1e30


def _cparams(sem):
    return pltpu.CompilerParams(dimension_semantics=sem, vmem_limit_bytes=VMEM_LIMIT)


def _silu(v):
    return v * jax.nn.sigmoid(v)


def _split3(v):
    a = v.astype(BF16)
    r = v - a.astype(F32)
    b = r.astype(BF16)
    c = (r - b.astype(F32)).astype(BF16)
    return a, b, c


def _dot01(m01, v):
    a, b, c = _split3(v)
    d = lambda t: jnp.dot(m01, t, preferred_element_type=F32)
    return d(a) + d(b) + d(c)


def _dot_split(v, v_hi, w_ref):
    v_lo = (v - v_hi.astype(F32)).astype(BF16)
    d = lambda a, b: jnp.dot(a, b, preferred_element_type=F32)
    return d(v_hi, w_ref[0]) + (d(v_hi, w_ref[1]) + d(v_lo, w_ref[0]))


def _split_hi_lo(w):
    hi = w.astype(BF16)
    return jnp.stack([hi, (w - hi.astype(F32)).astype(BF16)])


def _dot_nt(a, b):
    return lax.dot_general(a, b, (((1,), (1,)), ((), ())), preferred_element_type=F32)


def _dot_tn(a, b):
    return lax.dot_general(a, b, (((0,), (0,)), ((), ())), preferred_element_type=F32)


def _layer_norm(z, g, b):
    mu = jnp.mean(z, axis=-1, keepdims=True)
    zc = z - mu
    var = jnp.mean(zc * zc, axis=-1, keepdims=True)
    return zc * lax.rsqrt(var + LN_EPS) * g + b


def _mod_kernel(c_ref, w_ref, b_ref, o_ref):
    ca = _silu(c_ref[...])
    o_ref[0] = jnp.dot(ca, w_ref[0], precision=HIGHEST, preferred_element_type=F32) + b_ref[0]


def ada_mod(c, ada_w, ada_b):
    depth, d, d6 = ada_w.shape
    b = c.shape[0]
    tn = min(d6, 1536)
    return pl.pallas_call(
        _mod_kernel,
        grid=(depth, d6 // tn),
        in_specs=[pl.BlockSpec((b, d), lambda l, j: (0, 0)),
                  pl.BlockSpec((1, d, tn), lambda l, j: (l, 0, j)),
                  pl.BlockSpec((1, 1, tn), lambda l, j: (l, 0, j))],
        out_specs=pl.BlockSpec((1, b, tn), lambda l, j: (l, 0, j)),
        out_shape=jax.ShapeDtypeStruct((depth, b, d6), F32),
        compiler_params=_cparams(("parallel", "parallel")),
        name="ada_mod",
    )(c, ada_w, ada_b.reshape(depth, 1, d6))


def _in_proj_kernel(x_ref, mod_ref, w_ref, waf_ref, hq_ref, hf_ref, hi_ref, hg_ref,
                    aq_ref, ak_ref, av_ref, af_ref, *, d):
    shift = mod_ref[0, 0:1, :]
    scale = mod_ref[0, 1:2, :]
    h = x_ref[...] * (1.0 + scale) + shift
    hb = h.astype(BF16)

    def proj(g):
        return jnp.dot(hb, w_ref[:, g * d:(g + 1) * d], preferred_element_type=F32)

    hq_ref[...] = _silu(proj(0)).astype(BF16)
    hf_ref[...] = jax.nn.sigmoid(proj(1))
    hi_ref[...] = proj(2).astype(BF16)
    hg_ref[...] = _silu(proj(3)).astype(BF16)
    aq_ref[...] = (proj(4) * (FOX_DH ** -0.5 * LOG2E)).astype(BF16)
    ak_ref[...] = proj(5).astype(BF16)
    av_ref[...] = proj(6).astype(BF16)
    af_ref[...] = _dot_split(h, hb, waf_ref)


def in_proj(x2, mod, w7, waf, seq):
    n, d = x2.shape
    tpb = seq // TM
    row = lambda i: (i, 0)
    big = lambda dt: jax.ShapeDtypeStruct((n, d), dt)
    return pl.pallas_call(
        functools.partial(_in_proj_kernel, d=d),
        grid=(n // TM,),
        in_specs=[pl.BlockSpec((TM, d), row),
                  pl.BlockSpec((1, 6, d), lambda i: (i // tpb, 0, 0)),
                  pl.BlockSpec((d, 7 * d), lambda i: (0, 0), pipeline_mode=pl.Buffered(1)),
                  pl.BlockSpec((2, d, LANES), lambda i: (0, 0, 0), pipeline_mode=pl.Buffered(1))],
        out_specs=[pl.BlockSpec((TM, d), row)] * 7 + [pl.BlockSpec((TM, LANES), row)],
        out_shape=[big(BF16), big(F32), big(BF16), big(BF16), big(BF16), big(BF16), big(BF16),
                   jax.ShapeDtypeStruct((n, LANES), F32)],
        compiler_params=_cparams(("parallel",)),
        name="in_proj",
    )(x2, mod, w7, waf)


def _fox_decay_kernel(af_ref, bias_ref, o_ref, *, seq):
    heads = af_ref.shape[1]
    r = lax.broadcasted_iota(jnp.int32, (LANES, LANES), 0)
    c = lax.broadcasted_iota(jnp.int32, (LANES, LANES), 1)
    upper = jnp.where(r <= c, 1.0, 0.0).astype(BF16)
    carry = jnp.zeros((heads, 1), F32)
    for j in range(seq // LANES):
        z = af_ref[0, :, j * LANES:(j + 1) * LANES] + bias_ref[...]
        ls = jnp.minimum(z, 0.0) - jnp.log(1.0 + jnp.exp(-jnp.abs(z)))
        cs = _dot01_right(ls, upper) + carry
        o_ref[0, :, 0, j * LANES:(j + 1) * LANES] = cs * (-LOG2E)
        carry = cs[:, LANES - 1:LANES]


def _dot01_right(v, m01):
    a, b, c = _split3(v)
    d = lambda t: jnp.dot(t, m01, preferred_element_type=F32)
    return d(a) + d(b) + d(c)


def fox_decay(af_t, bias):
    b, heads, seq = af_t.shape
    return pl.pallas_call(
        functools.partial(_fox_decay_kernel, seq=seq),
        grid=(b,),
        in_specs=[pl.BlockSpec((1, heads, seq), lambda i: (i, 0, 0)),
                  pl.BlockSpec((heads, 1), lambda i: (0, 0))],
        out_specs=pl.BlockSpec((1, heads, 1, seq), lambda i: (i, 0, 0, 0)),
        out_shape=jax.ShapeDtypeStruct((b, heads, 1, seq), F32),
        compiler_params=_cparams(("parallel",)),
        name="fox_decay",
    )(af_t, bias.reshape(heads, 1))


def _hgrn_kernel(q_ref, f_ref, i_ref, g_ref, lbl_ref, nw_ref, o_ref, st_ref, c_ref, *, layer, ts, hps):
    C, SUB = HG_CHUNK, HG_SUB
    nsub = C // SUB

    @pl.when(pl.program_id(2) == 0)
    def _():
        st_ref[...] = jnp.zeros_like(st_ref)

    lg = lbl_ref[...]
    depth = lg.shape[0]
    rows = [lg[i:i + 1, :] for i in range(depth)]
    mx = functools.reduce(jnp.maximum, rows)
    ex = [jnp.exp(rw - mx) for rw in rows]
    den = functools.reduce(lambda a, b: a + b, ex)
    lb_all = jnp.zeros_like(mx)
    for i in range(1, layer + 1):
        lb_all = lb_all + ex[i] / den
    lb_all = jnp.clip(lb_all, 0.0, 1.0)
    nw = nw_ref[...]

    rr = lax.broadcasted_iota(jnp.int32, (C, C), 0)
    cc = lax.broadcasted_iota(jnp.int32, (C, C), 1)
    tril = jnp.where(cc <= rr, 1.0, 0.0).astype(BF16)
    sub8 = lax.broadcasted_iota(jnp.int32, (8, LANES), 0)
    lane8 = lax.broadcasted_iota(jnp.int32, (8, LANES), 1)
    zeros_sub = jnp.zeros((SUB, LANES), F32)

    def chunk_head(r0, hh):
        sl = slice(hh * HG_DK, (hh + 1) * HG_DK)
        lb = lb_all[:, sl]
        one_m_lb = 1.0 - lb
        qs = q_ref[pl.ds(r0, C), sl].astype(F32)
        sig = f_ref[pl.ds(r0, C), sl]
        iv = i_ref[pl.ds(r0, C), sl]
        gs = g_ref[pl.ds(r0, C), sl].astype(F32)
        f = lb + one_m_lb * sig
        g = jnp.log2(jnp.maximum(f, LOG_FLOOR))
        k = one_m_lb * (1.0 - sig)
        b = _dot01(tril, g)
        r_end = [b[(j + 1) * SUB - 1:(j + 1) * SUB, :] for j in range(nsub)]
        bc = lambda v: jnp.broadcast_to(v, (SUB, LANES))
        r_prev = jnp.concatenate([zeros_sub] + [bc(r_end[j]) for j in range(nsub - 1)], axis=0)
        r_own = jnp.concatenate([bc(r_end[j]) for j in range(nsub)], axis=0)
        r_last = r_end[nsub - 1]
        e_prev = jnp.concatenate([jnp.ones((SUB, LANES), F32)]
                                 + [bc(jnp.exp2(r_end[j])) for j in range(nsub - 1)], axis=0)
        e_tail = jnp.concatenate([bc(jnp.exp2(r_last - r_end[j])) for j in range(nsub)], axis=0)
        qhat = qs * jnp.exp2(b - r_prev)
        khat = k * jnp.exp2(r_own - b)
        qtil = qhat * e_prev
        kst = khat * e_tail

        lhs, rhs = [], []
        for j in range(nsub - 1):
            lo = (j + 1) * SUB
            qj = qs[lo:, :] * jnp.exp2(b[lo:, :] - r_end[j])
            lhs.append(jnp.concatenate([jnp.zeros((lo, LANES), F32), qj], axis=0))
            parts = []
            if j > 0:
                parts.append(jnp.zeros((j * SUB, LANES), F32))
            parts.append(khat[j * SUB:(j + 1) * SUB, :])
            parts.append(jnp.zeros((C - (j + 1) * SUB, LANES), F32))
            rhs.append(jnp.concatenate(parts, axis=0))
        a_off = _dot_nt(jnp.concatenate(lhs, axis=1).astype(BF16),
                        jnp.concatenate(rhs, axis=1).astype(BF16))

        c_ref[hh] = b - jnp.log2(k)
        pieces = []
        for blk in range(nsub):
            base = blk * SUB
            bt = [b[base:base + 8, :], b[base + 8:base + 16, :]]
            qt = [qs[base:base + 8, :], qs[base + 8:base + 16, :]]
            acc = [jnp.zeros((8, LANES), F32), jnp.zeros((8, LANES), F32)]
            for s in range(SUB):
                cs = jnp.broadcast_to(c_ref[hh, base + s:base + s + 1, :], (8, LANES))
                for half in range(2):
                    if s >= 8 and half == 0:
                        continue
                    d = bt[half] - cs
                    s_loc = s - 8 * half
                    if s_loc >= 0:
                        d = jnp.where(sub8 >= s_loc, d, NEG_BIG)
                    xval = jnp.exp2(d) * qt[half]
                    col = jnp.sum(xval, axis=-1, keepdims=True)
                    acc[half] = jnp.where(lane8 == base + s, col, acc[half])
            pieces += acc
        a_diag = jnp.concatenate(pieces, axis=0)
        a = (a_off + a_diag[:, :C]).astype(BF16)

        st = st_ref[hh]
        o = jnp.dot(a, iv, preferred_element_type=F32) + _dot_nt(qtil.astype(BF16), st.astype(BF16))
        st_ref[hh] = st * jnp.exp2(r_last) + _dot_tn(iv, kst.astype(BF16))

        ms = jnp.mean(o * o, axis=-1, keepdims=True)
        o = o * lax.rsqrt(ms + RMS_EPS) * nw
        o_ref[pl.ds(r0, C), sl] = (o * gs).astype(BF16)

    def chunk(ci, carry):
        r0 = pl.multiple_of(ci * C, C)
        for hh in range(hps):
            chunk_head(r0, hh)
        return carry

    lax.fori_loop(0, ts // C, chunk, 0)


def hgrn2(hq, hf, hi, hg, lb_logits, norm_w, *, layer, batch, seq):
    n, width = hq.shape
    hps = HG_HEADS_PER_STEP
    groups = width // (hps * HG_DK)
    ts = min(HG_TS, seq)
    spb = seq // ts
    depth = lb_logits.shape[0]
    blk = pl.BlockSpec((ts, hps * HG_DK), lambda b, h, t: (b * spb + t, h))
    return pl.pallas_call(
        functools.partial(_hgrn_kernel, layer=layer, ts=ts, hps=hps),
        grid=(batch, groups, spb),
        in_specs=[blk, blk, blk, blk,
                  pl.BlockSpec((depth, hps * HG_DK), lambda b, h, t: (0, h)),
                  pl.BlockSpec((1, HG_DK), lambda b, h, t: (0, 0))],
        out_specs=blk,
        out_shape=jax.ShapeDtypeStruct((n, width), BF16),
        scratch_shapes=[pltpu.VMEM((hps, HG_DK, HG_DK), F32), pltpu.VMEM((hps, HG_CHUNK, HG_DK), F32)],
        compiler_params=_cparams(("parallel", "parallel", "arbitrary")),
        name="hgrn2",
    )(hq, hf, hi, hg, lb_logits, norm_w.reshape(1, HG_DK))


def _fox_kernel(q_ref, k_ref, v_ref, nf_ref, o_ref, vaug_ref, acc_ref, m_ref, *, t, seq):
    qi = pl.program_id(2)

    @pl.when(qi == 0)
    def _():
        vaug_ref[:, :LANES] = v_ref[...]
        vaug_ref[:, LANES:] = jnp.ones((seq, LANES), BF16)

    lane = lax.broadcasted_iota(jnp.int32, (t, LANES), 1)
    q = q_ref[...]
    zero = jnp.zeros_like(q)
    qh = [jnp.where(lane < FOX_DH, q, zero), jnp.where(lane >= FOX_DH, q, zero)]
    for h in range(2):
        acc_ref[h] = jnp.zeros((t, 2 * LANES), F32)
        m_ref[h] = jnp.full((t, LANES), -jnp.inf, F32)
    row = lax.broadcasted_iota(jnp.int32, (t, t), 0)
    col = lax.broadcasted_iota(jnp.int32, (t, t), 1)

    def step(j, masked):
        c0 = pl.multiple_of(j * t, t)
        kb = k_ref[pl.ds(c0, t), :]
        vb = vaug_ref[pl.ds(c0, t), :]
        zs = []
        for h in range(2):
            z = _dot_nt(qh[h], kb) + nf_ref[0, h, :, pl.ds(c0, t)]
            if masked:
                z = jnp.where(col <= row, z, MASK_VALUE)
            zs.append(z)
        for h in range(2):
            z = zs[h]
            m_old = m_ref[h]
            m_new = jnp.maximum(m_old, jnp.max(z, axis=-1, keepdims=True))
            alpha = jnp.exp2(m_old - m_new)
            p = jnp.exp2((z - jnp.concatenate([m_new] * (t // LANES), axis=1)).astype(BF16))
            pv = jnp.dot(p, vb, preferred_element_type=F32)
            acc_ref[h] = jnp.concatenate([alpha, alpha], axis=1) * acc_ref[h] + pv
            m_ref[h] = m_new

    def body(j, carry):
        step(j, False)
        return carry

    lax.fori_loop(0, qi, body, 0)
    step(qi, True)
    a0 = acc_ref[0]
    a1 = acc_ref[1]
    o0 = a0[:, :LANES] / a0[:, LANES:]
    o1 = a1[:, :LANES] / a1[:, LANES:]
    o_ref[...] = jnp.where(lane < FOX_DH, o0, o1).astype(BF16)


def fox_attn(aq, ak, av, negf, *, batch, seq):
    n, width = aq.shape
    pairs = width // LANES
    t = min(FOX_T, seq)
    nq = seq // t
    return pl.pallas_call(
        functools.partial(_fox_kernel, t=t, seq=seq),
        grid=(batch, pairs, nq),
        in_specs=[pl.BlockSpec((t, LANES), lambda b, p, i: (b * nq + i, p)),
                  pl.BlockSpec((seq, LANES), lambda b, p, i: (b, p)),
                  pl.BlockSpec((seq, LANES), lambda b, p, i: (b, p)),
                  pl.BlockSpec((1, 2, 1, seq), lambda b, p, i: (b, p, 0, 0))],
        out_specs=pl.BlockSpec((t, LANES), lambda b, p, i: (b * nq + i, p)),
        out_shape=jax.ShapeDtypeStruct((n, width), BF16),
        scratch_shapes=[pltpu.VMEM((seq, 2 * LANES), BF16), pltpu.VMEM((2, t, 2 * LANES), F32),
                        pltpu.VMEM((2, t, LANES), F32)],
        compiler_params=_cparams(("parallel", "parallel", "arbitrary")),
        name="fox_attn",
    )(aq, ak, av, negf)


def _mixer_out_kernel(oa_ref, ob_ref, x_ref, mod_ref, wg_ref, wbr_ref, wo_ref, ln_ref, wr_ref, br_ref,
                      x1_ref, h2_ref, lg_ref, *, d, alpha):
    shift1, scale1, gate1 = mod_ref[0, 0:1, :], mod_ref[0, 1:2, :], mod_ref[0, 2:3, :]
    shift2, scale2 = mod_ref[0, 3:4, :], mod_ref[0, 4:5, :]
    x = x_ref[...]
    hb = (x * (1.0 + scale1) + shift1).astype(BF16)
    ga = jnp.dot(hb, wg_ref[:, :d], preferred_element_type=F32)
    gb = jnp.dot(hb, wg_ref[:, d:], preferred_element_type=F32)
    pa = jnp.dot(oa_ref[...], wbr_ref[:d, :], preferred_element_type=F32)
    pb = jnp.dot(ob_ref[...], wbr_ref[d:, :], preferred_element_type=F32)
    merged = jax.nn.sigmoid(ga) * pa + jax.nn.sigmoid(gb) * pb
    y = jnp.dot(merged.astype(BF16), wo_ref[...], preferred_element_type=F32)
    x1 = _layer_norm(alpha * x + (1.0 + gate1) * y, ln_ref[0:1, :], ln_ref[1:2, :])
    x1_ref[...] = x1
    h2 = x1 * (1.0 + scale2) + shift2
    h2b = h2.astype(BF16)
    h2_ref[...] = h2b
    lg_ref[...] = _dot_split(h2, h2b, wr_ref) + br_ref[...]


def mixer_out(oa, ob, x2, mod, wg, wbr, wo, ln, wr, br, *, seq, alpha):
    n, d = x2.shape
    tpb = seq // TM
    row = lambda i: (i, 0)
    const = lambda shape: pl.BlockSpec(shape, lambda i: (0, 0), pipeline_mode=pl.Buffered(1))
    return pl.pallas_call(
        functools.partial(_mixer_out_kernel, d=d, alpha=alpha),
        grid=(n // TM,),
        in_specs=[pl.BlockSpec((TM, d), row), pl.BlockSpec((TM, d), row), pl.BlockSpec((TM, d), row),
                  pl.BlockSpec((1, 6, d), lambda i: (i // tpb, 0, 0)),
                  const((d, 2 * d)), const((2 * d, d)), const((d, d)), const((2, d)),
                  pl.BlockSpec((2, d, LANES), lambda i: (0, 0, 0), pipeline_mode=pl.Buffered(1)),
                  const((1, LANES))],
        out_specs=[pl.BlockSpec((TM, d), row), pl.BlockSpec((TM, d), row), pl.BlockSpec((TM, LANES), row)],
        out_shape=[jax.ShapeDtypeStruct((n, d), F32), jax.ShapeDtypeStruct((n, d), BF16),
                   jax.ShapeDtypeStruct((n, LANES), F32)],
        compiler_params=_cparams(("parallel",)),
        name="mixer_out",
    )(oa, ob, x2, mod, wg, wbr, wo, ln, wr, br)


def _tile_rows(t):
    rows = t * TOP_K + N_EXPERTS * (CHUNK - 1)
    return -(-rows // (8 * CHUNK)) * (8 * CHUNK)


def _dispatch_kernel(lg_ref, h2_ref, xs_ref, rw_ref, cnt_ref, *, t, rows):
    lane = lax.broadcasted_iota(jnp.int32, (t, LANES), 1).astype(F32)
    lg = lg_ref[...]
    onehots, vals = [], []
    for _ in range(TOP_K):
        mx = jnp.max(lg, axis=-1, keepdims=True)
        idx = jnp.min(jnp.where(lg == mx, lane, float(LANES)), axis=-1, keepdims=True)
        oh = lane == idx
        onehots.append(oh)
        vals.append(mx)
        lg = jnp.where(oh, -jnp.inf, lg)
    es = [jnp.exp(v - vals[0]) for v in vals]
    den = functools.reduce(lambda a, b: a + b, es)
    ws = [e / den for e in es]

    member = functools.reduce(lambda a, b: a + b, [jnp.where(oh, 1.0, 0.0) for oh in onehots])
    rr = lax.broadcasted_iota(jnp.int32, (t, t), 0)
    cc = lax.broadcasted_iota(jnp.int32, (t, t), 1)
    strict_lower = jnp.where(cc < rr, 1.0, 0.0).astype(BF16)
    rank = jnp.dot(strict_lower, member.astype(BF16), preferred_element_type=F32)
    cnt = jnp.sum(member, axis=0, keepdims=True)
    nch = jnp.floor((cnt + (CHUNK - 1)) * (1.0 / CHUNK))
    ur = lax.broadcasted_iota(jnp.int32, (LANES, LANES), 0)
    uc = lax.broadcasted_iota(jnp.int32, (LANES, LANES), 1)
    strict_upper = jnp.where(ur < uc, 1.0, 0.0).astype(BF16)
    toff = jnp.dot(jnp.broadcast_to(nch, (8, LANES)).astype(BF16), strict_upper,
                   preferred_element_type=F32)[0:1, :]
    pos = toff * float(CHUNK) + rank

    rw = jnp.full((t, LANES), -1.0, F32)
    for kk in range(TOP_K):
        r_k = jnp.sum(jnp.where(onehots[kk], pos, 0.0), axis=-1, keepdims=True)
        rw = jnp.where(lane == kk, r_k, rw)
        rw = jnp.where(lane == TOP_K + kk, ws[kk], rw)
    rw_ref[...] = rw
    cnt_ref[0] = jnp.broadcast_to(cnt, (8, LANES))

    rt = rw.T
    riota = lax.broadcasted_iota(jnp.int32, (rows, t), 0).astype(F32)
    pm = jnp.zeros((rows, t), F32)
    for kk in range(TOP_K):
        pm = jnp.where(riota == rt[kk:kk + 1, :], 1.0, pm)
    xs_ref[...] = jnp.dot(pm.astype(BF16), h2_ref[...], preferred_element_type=F32).astype(BF16)


def dispatch(logits, h2):
    n, d = h2.shape
    t = min(ROUTE_T, n)
    rows = _tile_rows(t)
    nt = n // t
    return pl.pallas_call(
        functools.partial(_dispatch_kernel, t=t, rows=rows),
        grid=(nt,),
        in_specs=[pl.BlockSpec((t, LANES), lambda i: (i, 0)), pl.BlockSpec((t, d), lambda i: (i, 0))],
        out_specs=[pl.BlockSpec((rows, d), lambda i: (i, 0)), pl.BlockSpec((t, LANES), lambda i: (i, 0)),
                   pl.BlockSpec((1, 8, LANES), lambda i: (i, 0, 0))],
        out_shape=[jax.ShapeDtypeStruct((nt * rows, d), BF16), jax.ShapeDtypeStruct((n, LANES), F32),
                   jax.ShapeDtypeStruct((nt, 8, LANES), F32)],
        compiler_params=_cparams(("parallel",)),
        name="dispatch",
    )(logits, h2)


def _num_blocks(nt, t):
    max_chunks = nt * ((t * TOP_K + N_EXPERTS * (CHUNK - 1)) // CHUNK)
    return -(-max_chunks // BLOCK_CHUNKS) + N_EXPERTS


def _routing_tables(cnt, t):
    nt = cnt.shape[0]
    e = N_EXPERTS
    cpt = _tile_rows(t) // CHUNK
    nb = _num_blocks(nt, t)
    nch = (cnt + (CHUNK - 1)) // CHUNK
    toff = jnp.cumsum(nch, axis=1) - nch
    n_e = jnp.sum(nch, axis=0)
    nblk = (n_e + (BLOCK_CHUNKS - 1)) // BLOCK_CHUNKS
    bend = jnp.cumsum(nblk)
    bstart = bend - nblk
    total_blocks = bend[-1]
    cumj_incl = jnp.cumsum(nch, axis=0)
    cumj = cumj_incl - nch

    b = jnp.arange(nb, dtype=jnp.int32)
    last = jnp.maximum(total_blocks - 1, 0)
    bex = jnp.minimum(jnp.sum(bend[None, :] <= jnp.minimum(b, last)[:, None], axis=1), e - 1).astype(jnp.int32)
    bvalid = b < total_blocks
    sel = (bex[:, None] == jnp.arange(e, dtype=jnp.int32)[None, :]).astype(jnp.int32)
    row = lambda tab: jnp.sum(sel[:, :, None] * tab.T[None, :, :], axis=1)
    cum_b, excl_b, toff_b = row(cumj_incl), row(cumj), row(toff)
    n_b = jnp.sum(sel * n_e[None, :], axis=1)
    s0 = (b - jnp.sum(sel * bstart[None, :], axis=1)) * BLOCK_CHUNKS
    s = s0[:, None] + jnp.arange(BLOCK_CHUNKS, dtype=jnp.int32)[None, :]
    valid = bvalid[:, None] & (s < n_b[:, None])
    j = jnp.minimum(jnp.sum(cum_b[:, None, :] <= s[:, :, None], axis=2), nt - 1)
    jsel = (j[:, :, None] == jnp.arange(nt, dtype=jnp.int32)[None, None, :]).astype(jnp.int32)
    pick = lambda tab_b: jnp.sum(jsel * tab_b[:, None, :], axis=2)
    chunk = j * cpt + pick(toff_b) + (s - pick(excl_b))
    src = jnp.where(valid, chunk, 0).astype(jnp.int32).reshape(-1)
    dst = jnp.where(valid, chunk, -1).astype(jnp.int32).reshape(-1)
    return bex, bvalid.astype(jnp.int32), src, dst, nb


def _expert_kernel(bex_ref, bvalid_ref, src_ref, dst_ref, *refs, nb):
    xin = refs[:BLOCK_CHUNKS]
    wg_ref, bg_ref, wu_ref, bu_ref, wd_ref, bd_ref, ys_in_ref, ys_ref, xbuf, ybuf, wbf, sem = refs[BLOCK_CHUNKS:]
    del ys_in_ref
    i = pl.program_id(0)
    slot = i % 2

    def out_copy(step, sl, c):
        row = pl.multiple_of(dst_ref[step * BLOCK_CHUNKS + c] * CHUNK, CHUNK)
        return pltpu.make_async_copy(ybuf.at[sl, pl.ds(c * CHUNK, CHUNK), :],
                                     ys_ref.at[pl.ds(row, CHUNK), :], sem.at[sl])

    def wait_step(step, sl):
        for c in range(BLOCK_CHUNKS):
            @pl.when(dst_ref[step * BLOCK_CHUNKS + c] >= 0)
            def _():
                out_copy(step, sl, c).wait()

    @pl.when(i >= 2)
    def _():
        wait_step(i - 2, slot)

    prev = jnp.maximum(i - 1, 0)
    new_expert = (i == 0) | (bex_ref[i] != bex_ref[prev])

    @pl.when((bvalid_ref[i] == 1) & new_expert)
    def _():
        wbf[0] = wg_ref[0, 0].astype(BF16)
        wbf[1] = wu_ref[0, 0].astype(BF16)
        wbf[2] = wd_ref[0, 0].astype(BF16)

    @pl.when(bvalid_ref[i] == 1)
    def _():
        for c in range(BLOCK_CHUNKS):
            xbuf[c * CHUNK:(c + 1) * CHUNK, :] = xin[c][...]
        xb = xbuf[...]
        g = jnp.minimum(jnp.dot(xb, wbf[0], preferred_element_type=F32) + bg_ref[0, 0], SWIGLU_LIMIT)
        u = jnp.clip(jnp.dot(xb, wbf[1], preferred_element_type=F32) + bu_ref[0, 0], -SWIGLU_LIMIT, SWIGLU_LIMIT)
        a = (u + 1.0) * (g * jax.nn.sigmoid(SWIGLU_ALPHA * g))
        y = jnp.dot(a.astype(BF16), wbf[2], preferred_element_type=F32) + bd_ref[0, 0]
        ybuf[slot] = y.astype(BF16)
        for c in range(BLOCK_CHUNKS):
            @pl.when(dst_ref[i * BLOCK_CHUNKS + c] >= 0)
            def _():
                out_copy(i, slot, c).start()

    @pl.when(i == nb - 1)
    def _():
        @pl.when(i >= 1)
        def _():
            wait_step(i - 1, 1 - slot)
        wait_step(i, slot)


def experts(xs, ys_init, bex, bvalid, src, dst, w_gate, b_gate, w_up, b_up, w_down, b_down, *, nb, layer):
    d = xs.shape[1]
    depth, e, _, de = w_gate.shape
    rows = BLOCK_CHUNKS * CHUNK

    def chunk_spec(c):
        return pl.BlockSpec((CHUNK, d), lambda i, be, bv, sc, dc: (sc[i * BLOCK_CHUNKS + c], 0))

    wspec = lambda shape: pl.BlockSpec((1,) + shape, lambda i, be, bv, sc, dc: (layer, be[i], 0, 0))
    grid_spec = pltpu.PrefetchScalarGridSpec(
        num_scalar_prefetch=4,
        grid=(nb,),
        in_specs=[chunk_spec(c) for c in range(BLOCK_CHUNKS)] + [
            wspec((1, d, de)), wspec((1, 1, de)), wspec((1, d, de)), wspec((1, 1, de)),
            wspec((1, de, d)), wspec((1, 1, d)), pl.BlockSpec(memory_space=pl.ANY)],
        out_specs=pl.BlockSpec(memory_space=pl.ANY),
        scratch_shapes=[pltpu.VMEM((rows, d), BF16), pltpu.VMEM((2, rows, d), BF16),
                        pltpu.VMEM((3, d, de), BF16), pltpu.SemaphoreType.DMA((2,))],
    )
    n_in = 4 + BLOCK_CHUNKS + 6
    return pl.pallas_call(
        functools.partial(_expert_kernel, nb=nb),
        grid_spec=grid_spec,
        out_shape=jax.ShapeDtypeStruct(ys_init.shape, BF16),
        input_output_aliases={n_in: 0},
        compiler_params=_cparams(("arbitrary",)),
        name="experts",
    )(bex, bvalid, src, dst, *([xs] * BLOCK_CHUNKS),
      w_gate, b_gate.reshape(depth, e, 1, de), w_up, b_up.reshape(depth, e, 1, de),
      w_down, b_down.reshape(depth, e, 1, d), ys_init)


def _combine_kernel(ys_ref, rw_ref, x1_ref, mod_ref, ln_ref, o_ref, *, t, rows, alpha):
    gate2 = mod_ref[0, 5:6, :]
    rw = rw_ref[...]
    liota = lax.broadcasted_iota(jnp.int32, (t, rows), 1).astype(F32)
    wm = jnp.zeros((t, rows), F32)
    for kk in range(TOP_K):
        wm = jnp.where(liota == rw[:, kk:kk + 1], rw[:, TOP_K + kk:TOP_K + kk + 1], wm)
    y2 = jnp.dot(wm.astype(BF16), ys_ref[...], preferred_element_type=F32)
    o_ref[...] = _layer_norm(alpha * x1_ref[...] + (1.0 + gate2) * y2, ln_ref[0:1, :], ln_ref[1:2, :])


def combine(ys, rw, x1, mod, ln, *, seq, alpha):
    n, d = x1.shape
    t = min(ROUTE_T, n)
    rows = _tile_rows(t)
    tpb = max(seq // t, 1)
    return pl.pallas_call(
        functools.partial(_combine_kernel, t=t, rows=rows, alpha=alpha),
        grid=(n // t,),
        in_specs=[pl.BlockSpec((rows, d), lambda i: (i, 0)),
                  pl.BlockSpec((t, LANES), lambda i: (i, 0)),
                  pl.BlockSpec((t, d), lambda i: (i, 0)),
                  pl.BlockSpec((1, 6, d), lambda i: (i // tpb, 0, 0)),
                  pl.BlockSpec((2, d), lambda i: (0, 0))],
        out_specs=pl.BlockSpec((t, d), lambda i: (i, 0)),
        out_shape=jax.ShapeDtypeStruct((n, d), F32),
        compiler_params=_cparams(("parallel",)),
        name="combine",
    )(ys, rw, x1, mod, ln)


def moe_grouped(logits, h2, w_gate, b_gate, w_up, b_up, w_down, b_down, *, layer):
    n, d = h2.shape
    t = min(ROUTE_T, n)
    xs, rw, cnt = dispatch(logits, h2)
    cnt_i = cnt[:, 0, :N_EXPERTS].astype(jnp.int32)
    bex, bvalid, src, dst, nb = _routing_tables(cnt_i, t)
    ys_init = jnp.zeros(xs.shape, BF16)
    ys = experts(xs, ys_init, bex, bvalid, src, dst, w_gate, b_gate, w_up, b_up, w_down, b_down,
                 nb=nb, layer=layer)
    return ys, rw


def kernel(x, c, w_in, fox_f_bias, hg_lb_logits, hg_norm_w, w_branch, w_out, ada_w, ada_b,
           ln1_g, ln1_b, w_router, b_router, w_gate, b_gate, w_up, b_up, w_down, b_down,
           ln2_g, ln2_b):
    batch, seq, d = x.shape
    depth = w_in.shape[0]
    n = batch * seq
    alpha = (2 * depth) ** 0.25
    hgw = HG_HEADS * HG_DK
    foxw = FOX_HEADS * FOX_DH
    off_af = 4 * hgw + 3 * foxw
    off_g = off_af + FOX_HEADS

    mod_all = ada_mod(c, ada_w, ada_b).reshape(depth, batch, 6, d)
    x2 = x.reshape(n, d)

    for l in range(depth):
        mod = mod_all[l]
        w7 = w_in[l, :, :off_af].astype(BF16)
        waf = _split_hi_lo(jnp.pad(w_in[l, :, off_af:off_g], ((0, 0), (0, LANES - FOX_HEADS))))
        wg = w_in[l, :, off_g:].astype(BF16)
        hq, hf, hi, hg, aq, ak, av, af = in_proj(x2, mod, w7, waf, seq)

        af_t = af[:, :FOX_HEADS].reshape(batch, seq, FOX_HEADS).transpose(0, 2, 1)
        negf = fox_decay(af_t, fox_f_bias[l])
        o_a = hgrn2(hq, hf, hi, hg, hg_lb_logits, hg_norm_w[l], layer=l, batch=batch, seq=seq)
        o_b = fox_attn(aq, ak, av, negf, batch=batch, seq=seq)

        wr = _split_hi_lo(jnp.pad(w_router[l], ((0, 0), (0, LANES - N_EXPERTS))))
        br = jnp.pad(b_router[l], (0, LANES - N_EXPERTS), constant_values=NEG_BIG).reshape(1, LANES)
        x1, h2, logits = mixer_out(
            o_a, o_b, x2, mod, wg, w_branch[l].astype(BF16), w_out[l].astype(BF16),
            jnp.stack([ln1_g[l], ln1_b[l]]), wr, br, seq=seq, alpha=alpha)

        ys, rw = moe_grouped(logits, h2, w_gate, b_gate, w_up, b_up, w_down, b_down, layer=l)
        x2 = combine(ys, rw, x1, mod, jnp.stack([ln2_g[l], ln2_b[l]]), seq=seq, alpha=alpha)
    return x2.reshape(batch, seq, d)
```

```python
import functools
import math

import jax
import jax.numpy as jnp
from jax import lax
from jax.experimental import pallas as pl
from jax.experimental.pallas import tpu as pltpu

F32 = jnp.float32
BF16 = jnp.bfloat16
HIGHEST = lax.Precision.HIGHEST

HG_HEADS = 8
HG_DK = 128
FOX_HEADS = 16
FOX_DH = 64
N_EXPERTS = 32
TOP_K = 4
LOG_FLOOR = 1e-30
MASK_VALUE = -1e30
SWIGLU_LIMIT = 7.0
SWIGLU_ALPHA = 1.702
LN_EPS = 1e-5
RMS_EPS = 1e-6
LOG2E = math.log2(math.e)

LANES = 128
BF16_SUBLANES = 16
VMEM_LIMIT = 56 * 1024 * 1024

HG_CHUNK = 64
HG_SUB = 16
HG_TS = 1024
HG_HEADS_PER_STEP = 8
FOX_T = 512
TM = 512
ROUTE_T = 512
CHUNK = BF16_SUBLANES
BLOCK_CHUNKS = 32
NEG_BIG = -1e30


def _cparams(sem):
    return pltpu.CompilerParams(dimension_semantics=sem, vmem_limit_bytes=VMEM_LIMIT)


def _silu(v):
    return v * jax.nn.sigmoid(v)


def _split3(v):
    a = v.astype(BF16)
    r = v - a.astype(F32)
    b = r.astype(BF16)
    c = (r - b.astype(F32)).astype(BF16)
    return a, b, c


def _dot01(m01, v):
    a, b, c = _split3(v)
    d = lambda t: jnp.dot(m01, t, preferred_element_type=F32)
    return d(a) + d(b) + d(c)


def _dot_split(v, v_hi, w_ref):
    v_lo = (v - v_hi.astype(F32)).astype(BF16)
    d = lambda a, b: jnp.dot(a, b, preferred_element_type=F32)
    return d(v_hi, w_ref[0]) + (d(v_hi, w_ref[1]) + d(v_lo, w_ref[0]))


def _split_hi_lo(w):
    hi = w.astype(BF16)
    return jnp.stack([hi, (w - hi.astype(F32)).astype(BF16)])


def _dot_nt(a, b):
    return lax.dot_general(a, b, (((1,), (1,)), ((), ())), preferred_element_type=F32)


def _dot_tn(a, b):
    return lax.dot_general(a, b, (((0,), (0,)), ((), ())), preferred_element_type=F32)


def _layer_norm(z, g, b):
    mu = jnp.mean(z, axis=-1, keepdims=True)
    zc = z - mu
    var = jnp.mean(zc * zc, axis=-1, keepdims=True)
    return zc * lax.rsqrt(var + LN_EPS) * g + b


def _mod_kernel(c_ref, w_ref, b_ref, o_ref):
    ca = _silu(c_ref[...])
    o_ref[0] = jnp.dot(ca, w_ref[0], precision=HIGHEST, preferred_element_type=F32) + b_ref[0]


def ada_mod(c, ada_w, ada_b):
    depth, d, d6 = ada_w.shape
    b = c.shape[0]
    tn = min(d6, 1536)
    return pl.pallas_call(
        _mod_kernel,
        grid=(depth, d6 // tn),
        in_specs=[pl.BlockSpec((b, d), lambda l, j: (0, 0)),
                  pl.BlockSpec((1, d, tn), lambda l, j: (l, 0, j)),
                  pl.BlockSpec((1, 1, tn), lambda l, j: (l, 0, j))],
        out_specs=pl.BlockSpec((1, b, tn), lambda l, j: (l, 0, j)),
        out_shape=jax.ShapeDtypeStruct((depth, b, d6), F32),
        compiler_params=_cparams(("parallel", "parallel")),
        name="ada_mod",
    )(c, ada_w, ada_b.reshape(depth, 1, d6))


def _in_proj_kernel(x_ref, mod_ref, w_ref, waf_ref, sel_ref, hq_ref, hf_ref, hi_ref, hg_ref,
                    aq_ref, ak_ref, av_ref, af_ref, qq_ref, kk_ref, qk_ref, *, d):
    shift = mod_ref[0, 0:1, :]
    scale = mod_ref[0, 1:2, :]
    h = x_ref[...] * (1.0 + scale) + shift
    hb = h.astype(BF16)

    def proj(g):
        return jnp.dot(hb, w_ref[:, g * d:(g + 1) * d], preferred_element_type=F32)

    hq_ref[...] = _silu(proj(0)).astype(BF16)
    hf_ref[...] = jax.nn.sigmoid(proj(1))
    hi_ref[...] = proj(2).astype(BF16)
    hg_ref[...] = _silu(proj(3)).astype(BF16)
    qb = (proj(4) * (FOX_DH ** -0.5 * LOG2E)).astype(BF16)
    kb = proj(5).astype(BF16)
    aq_ref[...] = qb
    ak_ref[...] = kb
    av_ref[...] = proj(6).astype(BF16)
    af_ref[...] = _dot_split(h, hb, waf_ref)
    per_head = lambda v: jnp.dot(v, sel_ref[...], preferred_element_type=F32)
    qq_ref[...] = per_head(qb * qb)
    kk_ref[...] = per_head(kb * kb)
    qk_ref[...] = per_head(qb * kb)


def in_proj(x2, mod, w7, waf, seq):
    n, d = x2.shape
    tpb = seq // TM
    row = lambda i: (i, 0)
    big = lambda dt: jax.ShapeDtypeStruct((n, d), dt)
    small = jax.ShapeDtypeStruct((n, LANES), F32)
    sel = (jnp.arange(d)[:, None] // FOX_DH == jnp.arange(LANES)[None, :]).astype(BF16)
    return pl.pallas_call(
        functools.partial(_in_proj_kernel, d=d),
        grid=(n // TM,),
        in_specs=[pl.BlockSpec((TM, d), row),
                  pl.BlockSpec((1, 6, d), lambda i: (i // tpb, 0, 0)),
                  pl.BlockSpec((d, 7 * d), lambda i: (0, 0), pipeline_mode=pl.Buffered(1)),
                  pl.BlockSpec((2, d, LANES), lambda i: (0, 0, 0), pipeline_mode=pl.Buffered(1)),
                  pl.BlockSpec((d, LANES), lambda i: (0, 0), pipeline_mode=pl.Buffered(1))],
        out_specs=[pl.BlockSpec((TM, d), row)] * 7 + [pl.BlockSpec((TM, LANES), row)] * 4,
        out_shape=[big(BF16), big(F32), big(BF16), big(BF16), big(BF16), big(BF16), big(BF16),
                   small, small, small, small],
        compiler_params=_cparams(("parallel",)),
        name="in_proj",
    )(x2, mod, w7, waf, sel)


def _fox_decay_kernel(af_ref, bias_ref, qq_ref, kk_ref, qk_ref, nf_ref, st_ref, carry_ref, *, t, heads):
    @pl.when(pl.program_id(1) == 0)
    def _():
        carry_ref[...] = jnp.zeros_like(carry_ref)

    z = af_ref[...] + bias_ref[...]
    ls = jnp.minimum(z, 0.0) - jnp.log(1.0 + jnp.exp(-jnp.abs(z)))
    rr = lax.broadcasted_iota(jnp.int32, (t, t), 0)
    cc = lax.broadcasted_iota(jnp.int32, (t, t), 1)
    tril = jnp.where(cc <= rr, 1.0, 0.0).astype(BF16)
    cs = _dot01(tril, ls) + carry_ref[...]
    carry_ref[...] = cs[t - 1:t, :]
    nf = cs * (-LOG2E)
    nf_ref[0, :, 0, :] = nf.T[:heads, :]

    amax = jnp.max(jnp.sqrt(qq_ref[...]), axis=0, keepdims=True)
    kmax = jnp.max(jnp.sqrt(kk_ref[...]), axis=0, keepdims=True)
    dmin = jnp.min(qk_ref[...] + nf, axis=0, keepdims=True)
    st_ref[0, 0] = jnp.concatenate([amax, kmax, dmin, nf[t - 1:t, :], jnp.zeros((4, LANES), F32)], axis=0)


def fox_decay(af, bias, qq, kk, qk, *, batch, seq, heads):
    n = af.shape[0]
    t = min(FOX_T, seq)
    nblk = seq // t
    row = pl.BlockSpec((t, LANES), lambda b, i: (b * nblk + i, 0))
    return pl.pallas_call(
        functools.partial(_fox_decay_kernel, t=t, heads=heads),
        grid=(batch, nblk),
        in_specs=[row, pl.BlockSpec((1, LANES), lambda b, i: (0, 0)), row, row, row],
        out_specs=[pl.BlockSpec((1, heads, 1, t), lambda b, i: (b, 0, 0, i)),
                   pl.BlockSpec((1, 1, 8, LANES), lambda b, i: (b, i, 0, 0))],
        out_shape=[jax.ShapeDtypeStruct((batch, heads, 1, seq), F32),
                   jax.ShapeDtypeStruct((batch, nblk, 8, LANES), F32)],
        scratch_shapes=[pltpu.VMEM((1, LANES), F32)],
        compiler_params=_cparams(("parallel", "arbitrary")),
        name="fox_decay",
    )(af, jnp.pad(bias, (0, LANES - heads)).reshape(1, LANES), qq, kk, qk)


FOX_SKIP_LOG2 = 170.0


def fox_first_blocks(stats, *, heads):
    amax, kmax, dmin, e_end = (stats[:, :, r, :heads] for r in range(4))
    ub = amax[:, :, None, :] * kmax[:, None, :, :] * 1.02 + 2.0 - dmin[:, :, None, :] + e_end[:, None, :, :]
    nblk = stats.shape[1]
    earlier = jnp.arange(nblk)[None, :, None, None] > jnp.arange(nblk)[None, None, :, None]
    skip = (ub < -FOX_SKIP_LOG2) & earlier
    first = jnp.sum(jnp.cumprod(skip.astype(jnp.int32), axis=2), axis=2)
    first = jnp.min(first.reshape(first.shape[0], nblk, heads // 2, 2), axis=-1)
    return first.transpose(0, 2, 1).reshape(-1).astype(jnp.int32)


def _hgrn_kernel(q_ref, f_ref, i_ref, g_ref, lbl_ref, nw_ref, o_ref, st_ref, c_ref, *, layer, ts, hps):
    C, SUB = HG_CHUNK, HG_SUB
    nsub = C // SUB

    @pl.when(pl.program_id(2) == 0)
    def _():
        st_ref[...] = jnp.zeros_like(st_ref)

    lg = lbl_ref[...]
    depth = lg.shape[0]
    rows = [lg[i:i + 1, :] for i in range(depth)]
    mx = functools.reduce(jnp.maximum, rows)
    ex = [jnp.exp(rw - mx) for rw in rows]
    den = functools.reduce(lambda a, b: a + b, ex)
    lb_all = jnp.zeros_like(mx)
    for i in range(1, layer + 1):
        lb_all = lb_all + ex[i] / den
    lb_all = jnp.clip(lb_all, 0.0, 1.0)
    nw = nw_ref[...]

    rr = lax.broadcasted_iota(jnp.int32, (C, C), 0)
    cc = lax.broadcasted_iota(jnp.int32, (C, C), 1)
    tril = jnp.where(cc <= rr, 1.0, 0.0).astype(BF16)
    sub8 = lax.broadcasted_iota(jnp.int32, (8, LANES), 0)
    lane8 = lax.broadcasted_iota(jnp.int32, (8, LANES), 1)
    zeros_sub = jnp.zeros((SUB, LANES), F32)

    def chunk_head(r0, hh):
        sl = slice(hh * HG_DK, (hh + 1) * HG_DK)
        lb = lb_all[:, sl]
        one_m_lb = 1.0 - lb
        qs = q_ref[pl.ds(r0, C), sl].astype(F32)
        sig = f_ref[pl.ds(r0, C), sl]
        iv = i_ref[pl.ds(r0, C), sl]
        gs = g_ref[pl.ds(r0, C), sl].astype(F32)
        f = lb + one_m_lb * sig
        g = jnp.log2(jnp.maximum(f, LOG_FLOOR))
        k = one_m_lb * (1.0 - sig)
        b = _dot01(tril, g)
        r_end = [b[(j + 1) * SUB - 1:(j + 1) * SUB, :] for j in range(nsub)]
        bc = lambda v: jnp.broadcast_to(v, (SUB, LANES))
        r_prev = jnp.concatenate([zeros_sub] + [bc(r_end[j]) for j in range(nsub - 1)], axis=0)
        r_own = jnp.concatenate([bc(r_end[j]) for j in range(nsub)], axis=0)
        r_last = r_end[nsub - 1]
        e_prev = jnp.concatenate([jnp.ones((SUB, LANES), F32)]
                                 + [bc(jnp.exp2(r_end[j])) for j in range(nsub - 1)], axis=0)
        e_tail = jnp.concatenate([bc(jnp.exp2(r_last - r_end[j])) for j in range(nsub)], axis=0)
        qhat = qs * jnp.exp2(b - r_prev)
        khat = k * jnp.exp2(r_own - b)
        qtil = qhat * e_prev
        kst = khat * e_tail

        lhs, rhs = [], []
        for j in range(nsub - 1):
            lo = (j + 1) * SUB
            qj = qs[lo:, :] * jnp.exp2(b[lo:, :] - r_end[j])
            lhs.append(jnp.concatenate([jnp.zeros((lo, LANES), F32), qj], axis=0))
            parts = []
            if j > 0:
                parts.append(jnp.zeros((j * SUB, LANES), F32))
            parts.append(khat[j * SUB:(j + 1) * SUB, :])
            parts.append(jnp.zeros((C - (j + 1) * SUB, LANES), F32))
            rhs.append(jnp.concatenate(parts, axis=0))
        a_off = _dot_nt(jnp.concatenate(lhs, axis=1).astype(BF16),
                        jnp.concatenate(rhs, axis=1).astype(BF16))

        c_ref[hh] = b - jnp.log2(k)
        pieces = []
        for blk in range(nsub):
            base = blk * SUB
            bt = [b[base:base + 8, :], b[base + 8:base + 16, :]]
            qt = [qs[base:base + 8, :], qs[base + 8:base + 16, :]]
            acc = [jnp.zeros((8, LANES), F32), jnp.zeros((8, LANES), F32)]
            for s in range(SUB):
                cs = jnp.broadcast_to(c_ref[hh, base + s:base + s + 1, :], (8, LANES))
                for half in range(2):
                    if s >= 8 and half == 0:
                        continue
                    d = bt[half] - cs
                    s_loc = s - 8 * half
                    if s_loc >= 0:
                        d = jnp.where(sub8 >= s_loc, d, NEG_BIG)
                    xval = jnp.exp2(d) * qt[half]
                    col = jnp.sum(xval, axis=-1, keepdims=True)
                    acc[half] = jnp.where(lane8 == base + s, col, acc[half])
            pieces += acc
        a_diag = jnp.concatenate(pieces, axis=0)
        a = (a_off + a_diag[:, :C]).astype(BF16)

        st = st_ref[hh]
        o = jnp.dot(a, iv, preferred_element_type=F32) + _dot_nt(qtil.astype(BF16), st.astype(BF16))
        st_ref[hh] = st * jnp.exp2(r_last) + _dot_tn(iv, kst.astype(BF16))

        ms = jnp.mean(o * o, axis=-1, keepdims=True)
        o = o * lax.rsqrt(ms + RMS_EPS) * nw
        o_ref[pl.ds(r0, C), sl] = (o * gs).astype(BF16)

    def chunk(ci, carry):
        r0 = pl.multiple_of(ci * C, C)
        for hh in range(hps):
            chunk_head(r0, hh)
        return carry

    lax.fori_loop(0, ts // C, chunk, 0)


def hgrn2(hq, hf, hi, hg, lb_logits, norm_w, *, layer, batch, seq):
    n, width = hq.shape
    hps = HG_HEADS_PER_STEP
    groups = width // (hps * HG_DK)
    ts = min(HG_TS, seq)
    spb = seq // ts
    depth = lb_logits.shape[0]
    blk = pl.BlockSpec((ts, hps * HG_DK), lambda b, h, t: (b * spb + t, h))
    return pl.pallas_call(
        functools.partial(_hgrn_kernel, layer=layer, ts=ts, hps=hps),
        grid=(batch, groups, spb),
        in_specs=[blk, blk, blk, blk,
                  pl.BlockSpec((depth, hps * HG_DK), lambda b, h, t: (0, h)),
                  pl.BlockSpec((1, HG_DK), lambda b, h, t: (0, 0))],
        out_specs=blk,
        out_shape=jax.ShapeDtypeStruct((n, width), BF16),
        scratch_shapes=[pltpu.VMEM((hps, HG_DK, HG_DK), F32), pltpu.VMEM((hps, HG_CHUNK, HG_DK), F32)],
        compiler_params=_cparams(("parallel", "parallel", "arbitrary")),
        name="hgrn2",
    )(hq, hf, hi, hg, lb_logits, norm_w.reshape(1, HG_DK))


def _fox_kernel(first_ref, q_ref, k_ref, v_ref, nf_ref, o_ref, vaug_ref, acc_ref, m_ref, za_ref, zb_ref,
                *, t, seq):
    qi = pl.program_id(2)
    nq = pl.num_programs(2)
    first = first_ref[(pl.program_id(0) * pl.num_programs(1) + pl.program_id(1)) * nq + qi]

    @pl.when(qi == 0)
    def _():
        vaug_ref[:, :LANES] = v_ref[...]
        vaug_ref[:, LANES:] = jnp.ones((seq, LANES), BF16)

    lane = lax.broadcasted_iota(jnp.int32, (t, LANES), 1)
    q = q_ref[...]
    zero = jnp.zeros_like(q)
    qh = [jnp.where(lane < FOX_DH, q, zero), jnp.where(lane >= FOX_DH, q, zero)]
    for h in range(2):
        acc_ref[h] = jnp.zeros((t, 2 * LANES), F32)
        m_ref[h] = jnp.full((t, LANES), -jnp.inf, F32)
    row = lax.broadcasted_iota(jnp.int32, (t, t), 0)
    col = lax.broadcasted_iota(jnp.int32, (t, t), 1)

    def logits(j, z_ref):
        c0 = pl.multiple_of(j * t, t)
        kb = k_ref[pl.ds(c0, t), :]
        for h in range(2):
            z_ref[h] = _dot_nt(qh[h], kb) + nf_ref[0, h, :, pl.ds(c0, t)]

    def consume(j, z_ref, masked):
        c0 = pl.multiple_of(j * t, t)
        vb = vaug_ref[pl.ds(c0, t), :]
        for h in range(2):
            z = z_ref[h]
            if masked:
                z = jnp.where(col <= row, z, MASK_VALUE)
            m_old = m_ref[h]
            m_new = jnp.maximum(m_old, jnp.max(z, axis=-1, keepdims=True))
            alpha = jnp.exp2(m_old - m_new)
            p = jnp.exp2((z - jnp.concatenate([m_new] * (t // LANES), axis=1)).astype(BF16))
            pv = jnp.dot(p, vb, preferred_element_type=F32)
            acc_ref[h] = jnp.concatenate([alpha, alpha], axis=1) * acc_ref[h] + pv
            m_ref[h] = m_new

    start = jnp.minimum(first, qi)
    unmasked = qi - start
    logits(start, za_ref)

    def pair(p, carry):
        j = start + 2 * p
        logits(j + 1, zb_ref)
        consume(j, za_ref, False)
        logits(j + 2, za_ref)
        consume(j + 1, zb_ref, False)
        return carry

    lax.fori_loop(0, unmasked // 2, pair, 0)

    @pl.when(unmasked % 2 == 1)
    def _():
        logits(qi, zb_ref)
        consume(qi - 1, za_ref, False)
        consume(qi, zb_ref, True)

    @pl.when(unmasked % 2 == 0)
    def _():
        consume(qi, za_ref, True)

    a0 = acc_ref[0]
    a1 = acc_ref[1]
    o0 = a0[:, :LANES] / a0[:, LANES:]
    o1 = a1[:, :LANES] / a1[:, LANES:]
    o_ref[...] = jnp.where(lane < FOX_DH, o0, o1).astype(BF16)


def fox_attn(first, aq, ak, av, negf, *, batch, seq):
    n, width = aq.shape
    pairs = width // LANES
    t = min(FOX_T, seq)
    nq = seq // t
    grid_spec = pltpu.PrefetchScalarGridSpec(
        num_scalar_prefetch=1,
        grid=(batch, pairs, nq),
        in_specs=[pl.BlockSpec((t, LANES), lambda b, p, i, f: (b * nq + i, p)),
                  pl.BlockSpec((seq, LANES), lambda b, p, i, f: (b, p)),
                  pl.BlockSpec((seq, LANES), lambda b, p, i, f: (b, p)),
                  pl.BlockSpec((1, 2, 1, seq), lambda b, p, i, f: (b, p, 0, 0))],
        out_specs=pl.BlockSpec((t, LANES), lambda b, p, i, f: (b * nq + i, p)),
        scratch_shapes=[pltpu.VMEM((seq, 2 * LANES), BF16), pltpu.VMEM((2, t, 2 * LANES), F32),
                        pltpu.VMEM((2, t, LANES), F32), pltpu.VMEM((2, t, t), F32), pltpu.VMEM((2, t, t), F32)],
    )
    return pl.pallas_call(
        functools.partial(_fox_kernel, t=t, seq=seq),
        grid_spec=grid_spec,
        out_shape=jax.ShapeDtypeStruct((n, width), BF16),
        compiler_params=_cparams(("parallel", "parallel", "arbitrary")),
        name="fox_attn",
    )(first, aq, ak, av, negf)


def _mixer_out_kernel(oa_ref, ob_ref, x_ref, mod_ref, wg_ref, wbr_ref, wo_ref, ln_ref, wr_ref, br_ref,
                      x1_ref, h2_ref, lg_ref, *, d, alpha):
    shift1, scale1, gate1 = mod_ref[0, 0:1, :], mod_ref[0, 1:2, :], mod_ref[0, 2:3, :]
    shift2, scale2 = mod_ref[0, 3:4, :], mod_ref[0, 4:5, :]
    x = x_ref[...]
    hb = (x * (1.0 + scale1) + shift1).astype(BF16)
    ga = jnp.dot(hb, wg_ref[:, :d], preferred_element_type=F32)
    gb = jnp.dot(hb, wg_ref[:, d:], preferred_element_type=F32)
    pa = jnp.dot(oa_ref[...], wbr_ref[:d, :], preferred_element_type=F32)
    pb = jnp.dot(ob_ref[...], wbr_ref[d:, :], preferred_element_type=F32)
    merged = jax.nn.sigmoid(ga) * pa + jax.nn.sigmoid(gb) * pb
    y = jnp.dot(merged.astype(BF16), wo_ref[...], preferred_element_type=F32)
    x1 = _layer_norm(alpha * x + (1.0 + gate1) * y, ln_ref[0:1, :], ln_ref[1:2, :])
    x1_ref[...] = x1
    h2 = x1 * (1.0 + scale2) + shift2
    h2b = h2.astype(BF16)
    h2_ref[...] = h2b
    lg_ref[...] = _dot_split(h2, h2b, wr_ref) + br_ref[...]


def mixer_out(oa, ob, x2, mod, wg, wbr, wo, ln, wr, br, *, seq, alpha):
    n, d = x2.shape
    tpb = seq // TM
    row = lambda i: (i, 0)
    const = lambda shape: pl.BlockSpec(shape, lambda i: (0, 0), pipeline_mode=pl.Buffered(1))
    return pl.pallas_call(
        functools.partial(_mixer_out_kernel, d=d, alpha=alpha),
        grid=(n // TM,),
        in_specs=[pl.BlockSpec((TM, d), row), pl.BlockSpec((TM, d), row), pl.BlockSpec((TM, d), row),
                  pl.BlockSpec((1, 6, d), lambda i: (i // tpb, 0, 0)),
                  const((d, 2 * d)), const((2 * d, d)), const((d, d)), const((2, d)),
                  pl.BlockSpec((2, d, LANES), lambda i: (0, 0, 0), pipeline_mode=pl.Buffered(1)),
                  const((1, LANES))],
        out_specs=[pl.BlockSpec((TM, d), row), pl.BlockSpec((TM, d), row), pl.BlockSpec((TM, LANES), row)],
        out_shape=[jax.ShapeDtypeStruct((n, d), F32), jax.ShapeDtypeStruct((n, d), BF16),
                   jax.ShapeDtypeStruct((n, LANES), F32)],
        compiler_params=_cparams(("parallel",)),
        name="mixer_out",
    )(oa, ob, x2, mod, wg, wbr, wo, ln, wr, br)


def _tile_rows(t):
    rows = t * TOP_K + N_EXPERTS * (CHUNK - 1)
    return -(-rows // (8 * CHUNK)) * (8 * CHUNK)


def _dispatch_kernel(lg_ref, h2_ref, xs_ref, ys0_ref, rw_ref, cnt_ref, *, t, rows):
    ys0_ref[...] = jnp.zeros_like(ys0_ref)
    lane = lax.broadcasted_iota(jnp.int32, (t, LANES), 1).astype(F32)
    lg = lg_ref[...]
    onehots, vals = [], []
    for _ in range(TOP_K):
        mx = jnp.max(lg, axis=-1, keepdims=True)
        idx = jnp.min(jnp.where(lg == mx, lane, float(LANES)), axis=-1, keepdims=True)
        oh = lane == idx
        onehots.append(oh)
        vals.append(mx)
        lg = jnp.where(oh, -jnp.inf, lg)
    es = [jnp.exp(v - vals[0]) for v in vals]
    den = functools.reduce(lambda a, b: a + b, es)
    ws = [e / den for e in es]

    member = functools.reduce(lambda a, b: a + b, [jnp.where(oh, 1.0, 0.0) for oh in onehots])
    rr = lax.broadcasted_iota(jnp.int32, (t, t), 0)
    cc = lax.broadcasted_iota(jnp.int32, (t, t), 1)
    strict_lower = jnp.where(cc < rr, 1.0, 0.0).astype(BF16)
    rank = jnp.dot(strict_lower, member.astype(BF16), preferred_element_type=F32)
    cnt = jnp.sum(member, axis=0, keepdims=True)
    nch = jnp.floor((cnt + (CHUNK - 1)) * (1.0 / CHUNK))
    ur = lax.broadcasted_iota(jnp.int32, (LANES, LANES), 0)
    uc = lax.broadcasted_iota(jnp.int32, (LANES, LANES), 1)
    strict_upper = jnp.where(ur < uc, 1.0, 0.0).astype(BF16)
    toff = jnp.dot(jnp.broadcast_to(nch, (8, LANES)).astype(BF16), strict_upper,
                   preferred_element_type=F32)[0:1, :]
    pos = toff * float(CHUNK) + rank

    rw = jnp.full((t, LANES), -1.0, F32)
    for kk in range(TOP_K):
        r_k = jnp.sum(jnp.where(onehots[kk], pos, 0.0), axis=-1, keepdims=True)
        rw = jnp.where(lane == kk, r_k, rw)
        rw = jnp.where(lane == TOP_K + kk, ws[kk], rw)
    rw_ref[...] = rw
    cnt_ref[0] = jnp.broadcast_to(cnt, (8, LANES))

    rt = rw.T
    riota = lax.broadcasted_iota(jnp.int32, (rows, t), 0).astype(F32)
    pm = jnp.zeros((rows, t), F32)
    for kk in range(TOP_K):
        pm = jnp.where(riota == rt[kk:kk + 1, :], 1.0, pm)
    xs_ref[...] = jnp.dot(pm.astype(BF16), h2_ref[...], preferred_element_type=F32).astype(BF16)


def dispatch(logits, h2):
    n, d = h2.shape
    t = min(ROUTE_T, n)
    rows = _tile_rows(t)
    nt = n // t
    return pl.pallas_call(
        functools.partial(_dispatch_kernel, t=t, rows=rows),
        grid=(nt,),
        in_specs=[pl.BlockSpec((t, LANES), lambda i: (i, 0)), pl.BlockSpec((t, d), lambda i: (i, 0))],
        out_specs=[pl.BlockSpec((rows, d), lambda i: (i, 0)), pl.BlockSpec((rows, d), lambda i: (i, 0)),
                   pl.BlockSpec((t, LANES), lambda i: (i, 0)), pl.BlockSpec((1, 8, LANES), lambda i: (i, 0, 0))],
        out_shape=[jax.ShapeDtypeStruct((nt * rows, d), BF16), jax.ShapeDtypeStruct((nt * rows, d), BF16),
                   jax.ShapeDtypeStruct((n, LANES), F32), jax.ShapeDtypeStruct((nt, 8, LANES), F32)],
        compiler_params=_cparams(("parallel",)),
        name="dispatch",
    )(logits, h2)


def _num_blocks(nt, t):
    max_chunks = nt * ((t * TOP_K + N_EXPERTS * (CHUNK - 1)) // CHUNK)
    return -(-max_chunks // BLOCK_CHUNKS) + N_EXPERTS


def _routing_tables(cnt, t):
    nt = cnt.shape[0]
    e = N_EXPERTS
    cpt = _tile_rows(t) // CHUNK
    nb = _num_blocks(nt, t)
    nch = (cnt + (CHUNK - 1)) // CHUNK
    toff = jnp.cumsum(nch, axis=1) - nch
    n_e = jnp.sum(nch, axis=0)
    nblk = (n_e + (BLOCK_CHUNKS - 1)) // BLOCK_CHUNKS
    bend = jnp.cumsum(nblk)
    bstart = bend - nblk
    total_blocks = bend[-1]
    cumj_incl = jnp.cumsum(nch, axis=0)
    cumj = cumj_incl - nch

    b = jnp.arange(nb, dtype=jnp.int32)
    last = jnp.maximum(total_blocks - 1, 0)
    bex = jnp.minimum(jnp.sum(bend[None, :] <= jnp.minimum(b, last)[:, None], axis=1), e - 1).astype(jnp.int32)
    bvalid = b < total_blocks
    sel = (bex[:, None] == jnp.arange(e, dtype=jnp.int32)[None, :]).astype(jnp.int32)
    row = lambda tab: jnp.sum(sel[:, :, None] * tab.T[None, :, :], axis=1)
    cum_b, excl_b, toff_b = row(cumj_incl), row(cumj), row(toff)
    n_b = jnp.sum(sel * n_e[None, :], axis=1)
    s0 = (b - jnp.sum(sel * bstart[None, :], axis=1)) * BLOCK_CHUNKS
    s = s0[:, None] + jnp.arange(BLOCK_CHUNKS, dtype=jnp.int32)[None, :]
    valid = bvalid[:, None] & (s < n_b[:, None])
    j = jnp.minimum(jnp.sum(cum_b[:, None, :] <= s[:, :, None], axis=2), nt - 1)
    jsel = (j[:, :, None] == jnp.arange(nt, dtype=jnp.int32)[None, None, :]).astype(jnp.int32)
    pick = lambda tab_b: jnp.sum(jsel * tab_b[:, None, :], axis=2)
    chunk = j * cpt + pick(toff_b) + (s - pick(excl_b))
    src = jnp.where(valid, chunk, 0).astype(jnp.int32).reshape(-1)
    dst = jnp.where(valid, chunk, -1).astype(jnp.int32).reshape(-1)
    return bex, bvalid.astype(jnp.int32), src, dst, nb


def _expert_kernel(bex_ref, bvalid_ref, src_ref, dst_ref, *refs, nb):
    xin = refs[:BLOCK_CHUNKS]
    wg_ref, bg_ref, wu_ref, bu_ref, wd_ref, bd_ref, ys_in_ref, ys_ref, xbuf, ybuf, wbf, sem = refs[BLOCK_CHUNKS:]
    del ys_in_ref
    i = pl.program_id(0)
    slot = i % 2

    def out_copy(step, sl, c):
        row = pl.multiple_of(dst_ref[step * BLOCK_CHUNKS + c] * CHUNK, CHUNK)
        return pltpu.make_async_copy(ybuf.at[sl, pl.ds(c * CHUNK, CHUNK), :],
                                     ys_ref.at[pl.ds(row, CHUNK), :], sem.at[sl])

    def wait_step(step, sl):
        for c in range(BLOCK_CHUNKS):
            @pl.when(dst_ref[step * BLOCK_CHUNKS + c] >= 0)
            def _():
                out_copy(step, sl, c).wait()

    @pl.when(i >= 2)
    def _():
        wait_step(i - 2, slot)

    prev = jnp.maximum(i - 1, 0)
    new_expert = (i == 0) | (bex_ref[i] != bex_ref[prev])

    @pl.when((bvalid_ref[i] == 1) & new_expert)
    def _():
        wbf[0] = wg_ref[0, 0].astype(BF16)
        wbf[1] = wu_ref[0, 0].astype(BF16)
        wbf[2] = wd_ref[0, 0].astype(BF16)

    @pl.when(bvalid_ref[i] == 1)
    def _():
        for c in range(BLOCK_CHUNKS):
            xbuf[c * CHUNK:(c + 1) * CHUNK, :] = xin[c][...]
        xb = xbuf[...]
        g = jnp.minimum(jnp.dot(xb, wbf[0], preferred_element_type=F32) + bg_ref[0, 0], SWIGLU_LIMIT)
        u = jnp.clip(jnp.dot(xb, wbf[1], preferred_element_type=F32) + bu_ref[0, 0], -SWIGLU_LIMIT, SWIGLU_LIMIT)
        a = (u + 1.0) * (g * jax.nn.sigmoid(SWIGLU_ALPHA * g))
        y = jnp.dot(a.astype(BF16), wbf[2], preferred_element_type=F32) + bd_ref[0, 0]
        ybuf[slot] = y.astype(BF16)
        for c in range(BLOCK_CHUNKS):
            @pl.when(dst_ref[i * BLOCK_CHUNKS + c] >= 0)
            def _():
                out_copy(i, slot, c).start()

    @pl.when(i == nb - 1)
    def _():
        @pl.when(i >= 1)
        def _():
            wait_step(i - 1, 1 - slot)
        wait_step(i, slot)


def experts(xs, ys_init, bex, bvalid, src, dst, w_gate, b_gate, w_up, b_up, w_down, b_down, *, nb, layer):
    d = xs.shape[1]
    depth, e, _, de = w_gate.shape
    rows = BLOCK_CHUNKS * CHUNK

    def chunk_spec(c):
        return pl.BlockSpec((CHUNK, d), lambda i, be, bv, sc, dc: (sc[i * BLOCK_CHUNKS + c], 0))

    wspec = lambda shape: pl.BlockSpec((1,) + shape, lambda i, be, bv, sc, dc: (layer, be[i], 0, 0))
    grid_spec = pltpu.PrefetchScalarGridSpec(
        num_scalar_prefetch=4,
        grid=(nb,),
        in_specs=[chunk_spec(c) for c in range(BLOCK_CHUNKS)] + [
            wspec((1, d, de)), wspec((1, 1, de)), wspec((1, d, de)), wspec((1, 1, de)),
            wspec((1, de, d)), wspec((1, 1, d)), pl.BlockSpec(memory_space=pl.ANY)],
        out_specs=pl.BlockSpec(memory_space=pl.ANY),
        scratch_shapes=[pltpu.VMEM((rows, d), BF16), pltpu.VMEM((2, rows, d), BF16),
                        pltpu.VMEM((3, d, de), BF16), pltpu.SemaphoreType.DMA((2,))],
    )
    n_in = 4 + BLOCK_CHUNKS + 6
    return pl.pallas_call(
        functools.partial(_expert_kernel, nb=nb),
        grid_spec=grid_spec,
        out_shape=jax.ShapeDtypeStruct(ys_init.shape, BF16),
        input_output_aliases={n_in: 0},
        compiler_params=_cparams(("arbitrary",)),
        name="experts",
    )(bex, bvalid, src, dst, *([xs] * BLOCK_CHUNKS),
      w_gate, b_gate.reshape(depth, e, 1, de), w_up, b_up.reshape(depth, e, 1, de),
      w_down, b_down.reshape(depth, e, 1, d), ys_init)


def _combine_kernel(ys_ref, rw_ref, x1_ref, mod_ref, ln_ref, o_ref, *, t, rows, alpha):
    gate2 = mod_ref[0, 5:6, :]
    rw = rw_ref[...]
    liota = lax.broadcasted_iota(jnp.int32, (t, rows), 1).astype(F32)
    wm = jnp.zeros((t, rows), F32)
    for kk in range(TOP_K):
        wm = jnp.where(liota == rw[:, kk:kk + 1], rw[:, TOP_K + kk:TOP_K + kk + 1], wm)
    y2 = jnp.dot(wm.astype(BF16), ys_ref[...], preferred_element_type=F32)
    o_ref[...] = _layer_norm(alpha * x1_ref[...] + (1.0 + gate2) * y2, ln_ref[0:1, :], ln_ref[1:2, :])


def combine(ys, rw, x1, mod, ln, *, seq, alpha):
    n, d = x1.shape
    t = min(ROUTE_T, n)
    rows = _tile_rows(t)
    tpb = max(seq // t, 1)
    return pl.pallas_call(
        functools.partial(_combine_kernel, t=t, rows=rows, alpha=alpha),
        grid=(n // t,),
        in_specs=[pl.BlockSpec((rows, d), lambda i: (i, 0)),
                  pl.BlockSpec((t, LANES), lambda i: (i, 0)),
                  pl.BlockSpec((t, d), lambda i: (i, 0)),
                  pl.BlockSpec((1, 6, d), lambda i: (i // tpb, 0, 0)),
                  pl.BlockSpec((2, d), lambda i: (0, 0))],
        out_specs=pl.BlockSpec((t, d), lambda i: (i, 0)),
        out_shape=jax.ShapeDtypeStruct((n, d), F32),
        compiler_params=_cparams(("parallel",)),
        name="combine",
    )(ys, rw, x1, mod, ln)


def moe_grouped(logits, h2, w_gate, b_gate, w_up, b_up, w_down, b_down, *, layer):
    n, d = h2.shape
    t = min(ROUTE_T, n)
    xs, ys_init, rw, cnt = dispatch(logits, h2)
    cnt_i = cnt[:, 0, :N_EXPERTS].astype(jnp.int32)
    bex, bvalid, src, dst, nb = _routing_tables(cnt_i, t)
    ys = experts(xs, ys_init, bex, bvalid, src, dst, w_gate, b_gate, w_up, b_up, w_down, b_down,
                 nb=nb, layer=layer)
    return ys, rw


def kernel(x, c, w_in, fox_f_bias, hg_lb_logits, hg_norm_w, w_branch, w_out, ada_w, ada_b,
           ln1_g, ln1_b, w_router, b_router, w_gate, b_gate, w_up, b_up, w_down, b_down,
           ln2_g, ln2_b):
    batch, seq, d = x.shape
    depth = w_in.shape[0]
    n = batch * seq
    alpha = (2 * depth) ** 0.25
    hgw = HG_HEADS * HG_DK
    foxw = FOX_HEADS * FOX_DH
    off_af = 4 * hgw + 3 * foxw
    off_g = off_af + FOX_HEADS

    mod_all = ada_mod(c, ada_w, ada_b).reshape(depth, batch, 6, d)
    x2 = x.reshape(n, d)

    for l in range(depth):
        mod = mod_all[l]
        w7 = w_in[l, :, :off_af].astype(BF16)
        waf = _split_hi_lo(jnp.pad(w_in[l, :, off_af:off_g], ((0, 0), (0, LANES - FOX_HEADS))))
        wg = w_in[l, :, off_g:].astype(BF16)
        hq, hf, hi, hg, aq, ak, av, af, qq, kk, qk = in_proj(x2, mod, w7, waf, seq)

        negf, stats = fox_decay(af, fox_f_bias[l], qq, kk, qk, batch=batch, seq=seq, heads=FOX_HEADS)
        first = fox_first_blocks(stats, heads=FOX_HEADS)
        o_a = hgrn2(hq, hf, hi, hg, hg_lb_logits, hg_norm_w[l], layer=l, batch=batch, seq=seq)
        o_b = fox_attn(first, aq, ak, av, negf, batch=batch, seq=seq)

        wr = _split_hi_lo(jnp.pad(w_router[l], ((0, 0), (0, LANES - N_EXPERTS))))
        br = jnp.pad(b_router[l], (0, LANES - N_EXPERTS), constant_values=NEG_BIG).reshape(1, LANES)
        x1, h2, logits = mixer_out(
            o_a, o_b, x2, mod, wg, w_branch[l].astype(BF16), w_out[l].astype(BF16),
            jnp.stack([ln1_g[l], ln1_b[l]]), wr, br, seq=seq, alpha=alpha)

        ys, rw = moe_grouped(logits, h2, w_gate, b_gate, w_up, b_up, w_down, b_down, layer=l)
        x2 = combine(ys, rw, x1, mod, jnp.stack([ln2_g[l], ln2_b[l]]), seq=seq, alpha=alpha)
    return x2.reshape(batch, seq, d)
```

```python
import functools
import math

import jax
import jax.numpy as jnp
from jax import lax
from jax.experimental import pallas as pl
from jax.experimental.pallas import tpu as pltpu

F32 = jnp.float32
BF16 = jnp.bfloat16
HIGHEST = lax.Precision.HIGHEST

HG_HEADS = 8
HG_DK = 128
FOX_HEADS = 16
FOX_DH = 64
N_EXPERTS = 32
TOP_K = 4
LOG_FLOOR = 1e-30
MASK_VALUE = -1e30
SWIGLU_LIMIT = 7.0
SWIGLU_ALPHA = 1.702
LN_EPS = 1e-5
RMS_EPS = 1e-6
LOG2E = math.log2(math.e)

LANES = 128
BF16_SUBLANES = 16
VMEM_LIMIT = 56 * 1024 * 1024

HG_CHUNK = 64
HG_SUB = 16
HG_TS = 1024
HG_UNROLL = 2
HG_HEADS_PER_STEP = 8
FOX_T = 512
FOX_TK = 512
TM = 512
ROUTE_T = 512
CHUNK = BF16_SUBLANES
BLOCK_CHUNKS = 32
PERM_STRIP = 512
NEG_BIG = -1e30


def _cparams(sem):
    return pltpu.CompilerParams(dimension_semantics=sem, vmem_limit_bytes=VMEM_LIMIT)


def _silu(v):
    return v * jax.nn.sigmoid(v)


def _split3(v):
    a = v.astype(BF16)
    r = v - a.astype(F32)
    b = r.astype(BF16)
    c = (r - b.astype(F32)).astype(BF16)
    return a, b, c


def _dot01(m01, v):
    a, b, c = _split3(v)
    d = lambda t: jnp.dot(m01, t, preferred_element_type=F32)
    return d(a) + d(b) + d(c)


def _dot_split(v, v_hi, w_ref):
    v_lo = (v - v_hi.astype(F32)).astype(BF16)
    d = lambda a, b: jnp.dot(a, b, preferred_element_type=F32)
    return d(v_hi, w_ref[0]) + (d(v_hi, w_ref[1]) + d(v_lo, w_ref[0]))


def _split_hi_lo(w):
    hi = w.astype(BF16)
    return jnp.stack([hi, (w - hi.astype(F32)).astype(BF16)])


def _dot_nt(a, b):
    return lax.dot_general(a, b, (((1,), (1,)), ((), ())), preferred_element_type=F32)


def _dot_tn(a, b):
    return lax.dot_general(a, b, (((0,), (0,)), ((), ())), preferred_element_type=F32)


def _layer_norm(z, g, b):
    mu = jnp.mean(z, axis=-1, keepdims=True)
    zc = z - mu
    var = jnp.mean(zc * zc, axis=-1, keepdims=True)
    return zc * lax.rsqrt(var + LN_EPS) * g + b


def _mod_kernel(c_ref, w_ref, b_ref, o_ref):
    ca = _silu(c_ref[...])
    o_ref[0] = jnp.dot(ca, w_ref[0], precision=HIGHEST, preferred_element_type=F32) + b_ref[0]


def ada_mod(c, ada_w, ada_b):
    depth, d, d6 = ada_w.shape
    b = c.shape[0]
    tn = min(d6, 1536)
    return pl.pallas_call(
        _mod_kernel,
        grid=(depth, d6 // tn),
        in_specs=[pl.BlockSpec((b, d), lambda l, j: (0, 0)),
                  pl.BlockSpec((1, d, tn), lambda l, j: (l, 0, j)),
                  pl.BlockSpec((1, 1, tn), lambda l, j: (l, 0, j))],
        out_specs=pl.BlockSpec((1, b, tn), lambda l, j: (l, 0, j)),
        out_shape=jax.ShapeDtypeStruct((depth, b, d6), F32),
        compiler_params=_cparams(("parallel", "parallel")),
        name="ada_mod",
    )(c, ada_w, ada_b.reshape(depth, 1, d6))


def _in_proj_kernel(x_ref, mod_ref, w_ref, waf_ref, sel_ref, hq_ref, hf_ref, hi_ref, hg_ref,
                    aq_ref, ak_ref, av_ref, af_ref, qq_ref, kk_ref, qk_ref, *, d):
    shift = mod_ref[0, 0:1, :]
    scale = mod_ref[0, 1:2, :]
    h = x_ref[...] * (1.0 + scale) + shift
    hb = h.astype(BF16)

    def proj(g):
        return jnp.dot(hb, w_ref[:, g * d:(g + 1) * d], preferred_element_type=F32)

    hq_ref[...] = _silu(proj(0)).astype(BF16)
    hf_ref[...] = jax.nn.sigmoid(proj(1))
    hi_ref[...] = proj(2).astype(BF16)
    hg_ref[...] = _silu(proj(3)).astype(BF16)
    qb = (proj(4) * (FOX_DH ** -0.5 * LOG2E)).astype(BF16)
    kb = proj(5).astype(BF16)
    aq_ref[...] = qb
    ak_ref[...] = kb
    av_ref[...] = proj(6).astype(BF16)
    af_ref[...] = _dot_split(h, hb, waf_ref)
    per_head = lambda v: jnp.dot(v, sel_ref[...], preferred_element_type=F32)
    qq_ref[...] = per_head(qb * qb)
    kk_ref[...] = per_head(kb * kb)
    qk_ref[...] = per_head(qb * kb)


def in_proj(x2, mod, w7, waf, seq):
    n, d = x2.shape
    tpb = seq // TM
    row = lambda i: (i, 0)
    big = lambda dt: jax.ShapeDtypeStruct((n, d), dt)
    small = jax.ShapeDtypeStruct((n, LANES), F32)
    sel = (jnp.arange(d)[:, None] // FOX_DH == jnp.arange(LANES)[None, :]).astype(BF16)
    return pl.pallas_call(
        functools.partial(_in_proj_kernel, d=d),
        grid=(n // TM,),
        in_specs=[pl.BlockSpec((TM, d), row),
                  pl.BlockSpec((1, 6, d), lambda i: (i // tpb, 0, 0)),
                  pl.BlockSpec((d, 7 * d), lambda i: (0, 0), pipeline_mode=pl.Buffered(1)),
                  pl.BlockSpec((2, d, LANES), lambda i: (0, 0, 0), pipeline_mode=pl.Buffered(1)),
                  pl.BlockSpec((d, LANES), lambda i: (0, 0), pipeline_mode=pl.Buffered(1))],
        out_specs=[pl.BlockSpec((TM, d), row)] * 7 + [pl.BlockSpec((TM, LANES), row)] * 4,
        out_shape=[big(BF16), big(F32), big(BF16), big(BF16), big(BF16), big(BF16), big(BF16),
                   small, small, small, small],
        compiler_params=_cparams(("parallel",)),
        name="in_proj",
    )(x2, mod, w7, waf, sel)


def _fox_decay_kernel(af_ref, bias_ref, qq_ref, kk_ref, qk_ref, nf_ref, st_ref, carry_ref, *, t, tk, heads):
    @pl.when(pl.program_id(1) == 0)
    def _():
        carry_ref[...] = jnp.zeros_like(carry_ref)

    z = af_ref[...] + bias_ref[...]
    ls = jnp.minimum(z, 0.0) - jnp.log(1.0 + jnp.exp(-jnp.abs(z)))
    rr = lax.broadcasted_iota(jnp.int32, (t, t), 0)
    cc = lax.broadcasted_iota(jnp.int32, (t, t), 1)
    tril = jnp.where(cc <= rr, 1.0, 0.0).astype(BF16)
    cs = _dot01(tril, ls) + carry_ref[...]
    carry_ref[...] = cs[t - 1:t, :]
    nf = cs * (-LOG2E)
    nf_ref[0, :, 0, :] = nf.T[:heads, :]

    amax = jnp.max(jnp.sqrt(qq_ref[...]), axis=0, keepdims=True)
    dmin = jnp.min(qk_ref[...] + nf, axis=0, keepdims=True)
    kn = jnp.sqrt(kk_ref[...])
    sub = t // tk
    kmax = [jnp.max(kn[s * tk:(s + 1) * tk, :], axis=0, keepdims=True) for s in range(sub)]
    ends = [nf[(s + 1) * tk - 1:(s + 1) * tk, :] for s in range(sub)]
    pad = jnp.zeros((8 - 2 - 2 * sub, LANES), F32)
    st_ref[0, 0] = jnp.concatenate([amax, dmin] + kmax + ends + [pad], axis=0)


def fox_decay(af, bias, qq, kk, qk, *, batch, seq, heads):
    n = af.shape[0]
    t = min(FOX_T, seq)
    nblk = seq // t
    row = pl.BlockSpec((t, LANES), lambda b, i: (b * nblk + i, 0))
    return pl.pallas_call(
        functools.partial(_fox_decay_kernel, t=t, tk=min(FOX_TK, t), heads=heads),
        grid=(batch, nblk),
        in_specs=[row, pl.BlockSpec((1, LANES), lambda b, i: (0, 0)), row, row, row],
        out_specs=[pl.BlockSpec((1, heads, 1, t), lambda b, i: (b, 0, 0, i)),
                   pl.BlockSpec((1, 1, 8, LANES), lambda b, i: (b, i, 0, 0))],
        out_shape=[jax.ShapeDtypeStruct((batch, heads, 1, seq), F32),
                   jax.ShapeDtypeStruct((batch, nblk, 8, LANES), F32)],
        scratch_shapes=[pltpu.VMEM((1, LANES), F32)],
        compiler_params=_cparams(("parallel", "arbitrary")),
        name="fox_decay",
    )(af, jnp.pad(bias, (0, LANES - heads)).reshape(1, LANES), qq, kk, qk)


FOX_SKIP_LOG2 = 170.0


def fox_first_blocks(stats, *, heads, sub):
    nb, nblk = stats.shape[0], stats.shape[1]
    amax, dmin = stats[:, :, 0, :heads], stats[:, :, 1, :heads]
    kmax = stats[:, :, 2:2 + sub, :heads].reshape(nb, nblk * sub, heads)
    e_end = stats[:, :, 2 + sub:2 + 2 * sub, :heads].reshape(nb, nblk * sub, heads)
    ub = amax[:, :, None, :] * kmax[:, None, :, :] * 1.02 + 2.0 - dmin[:, :, None, :] + e_end[:, None, :, :]
    earlier = (jnp.arange(nblk) * sub)[None, :, None, None] > jnp.arange(nblk * sub)[None, None, :, None]
    skip = (ub < -FOX_SKIP_LOG2) & earlier
    first = jnp.sum(jnp.cumprod(skip.astype(jnp.int32), axis=2), axis=2)
    first = jnp.min(first.reshape(first.shape[0], nblk, heads // 2, 2), axis=-1)
    return first.transpose(0, 2, 1).reshape(-1).astype(jnp.int32)


def _hgrn_kernel(q_ref, f_ref, i_ref, g_ref, lbl_ref, nw_ref, o_ref, st_ref, c_ref, *, layer, ts, hps):
    C, SUB = HG_CHUNK, HG_SUB
    nsub = C // SUB

    @pl.when(pl.program_id(2) == 0)
    def _():
        st_ref[...] = jnp.zeros_like(st_ref)

    lg = lbl_ref[...]
    depth = lg.shape[0]
    rows = [lg[i:i + 1, :] for i in range(depth)]
    mx = functools.reduce(jnp.maximum, rows)
    ex = [jnp.exp(rw - mx) for rw in rows]
    den = functools.reduce(lambda a, b: a + b, ex)
    lb_all = jnp.zeros_like(mx)
    for i in range(1, layer + 1):
        lb_all = lb_all + ex[i] / den
    lb_all = jnp.clip(lb_all, 0.0, 1.0)
    nw = nw_ref[...]

    rr = lax.broadcasted_iota(jnp.int32, (C, C), 0)
    cc = lax.broadcasted_iota(jnp.int32, (C, C), 1)
    tril = jnp.where(cc <= rr, 1.0, 0.0).astype(BF16)
    sub8 = lax.broadcasted_iota(jnp.int32, (8, LANES), 0)
    lane8 = lax.broadcasted_iota(jnp.int32, (8, LANES), 1)
    zeros_sub = jnp.zeros((SUB, LANES), F32)

    def chunk_head(r0, hh, cslot):
        sl = slice(hh * HG_DK, (hh + 1) * HG_DK)
        lb = lb_all[:, sl]
        one_m_lb = 1.0 - lb
        qs = q_ref[pl.ds(r0, C), sl].astype(F32)
        sig = f_ref[pl.ds(r0, C), sl]
        iv = i_ref[pl.ds(r0, C), sl]
        gs = g_ref[pl.ds(r0, C), sl].astype(F32)
        f = lb + one_m_lb * sig
        g = jnp.log2(jnp.maximum(f, LOG_FLOOR))
        k = one_m_lb * (1.0 - sig)
        b = _dot01(tril, g)
        r_end = [b[(j + 1) * SUB - 1:(j + 1) * SUB, :] for j in range(nsub)]
        bc = lambda v: jnp.broadcast_to(v, (SUB, LANES))
        r_prev = jnp.concatenate([zeros_sub] + [bc(r_end[j]) for j in range(nsub - 1)], axis=0)
        r_own = jnp.concatenate([bc(r_end[j]) for j in range(nsub)], axis=0)
        r_last = r_end[nsub - 1]
        e_prev = jnp.concatenate([jnp.ones((SUB, LANES), F32)]
                                 + [bc(jnp.exp2(r_end[j])) for j in range(nsub - 1)], axis=0)
        e_tail = jnp.concatenate([bc(jnp.exp2(r_last - r_end[j])) for j in range(nsub)], axis=0)
        qhat = qs * jnp.exp2(b - r_prev)
        khat = k * jnp.exp2(r_own - b)
        qtil = qhat * e_prev
        kst = khat * e_tail

        lhs, rhs = [], []
        for j in range(nsub - 1):
            lo = (j + 1) * SUB
            qj = qs[lo:, :] * jnp.exp2(b[lo:, :] - r_end[j])
            lhs.append(jnp.concatenate([jnp.zeros((lo, LANES), F32), qj], axis=0))
            parts = []
            if j > 0:
                parts.append(jnp.zeros((j * SUB, LANES), F32))
            parts.append(khat[j * SUB:(j + 1) * SUB, :])
            parts.append(jnp.zeros((C - (j + 1) * SUB, LANES), F32))
            rhs.append(jnp.concatenate(parts, axis=0))
        a_off = _dot_nt(jnp.concatenate(lhs, axis=1).astype(BF16),
                        jnp.concatenate(rhs, axis=1).astype(BF16))

        c_ref[cslot] = b - jnp.log2(k)
        pieces = []
        for blk in range(nsub):
            base = blk * SUB
            bt = [b[base:base + 8, :], b[base + 8:base + 16, :]]
            qt = [qs[base:base + 8, :], qs[base + 8:base + 16, :]]
            acc = [jnp.zeros((8, LANES), F32), jnp.zeros((8, LANES), F32)]
            for s in range(SUB):
                cs = jnp.broadcast_to(c_ref[cslot, base + s:base + s + 1, :], (8, LANES))
                for half in range(2):
                    if s >= 8 and half == 0:
                        continue
                    d = bt[half] - cs
                    s_loc = s - 8 * half
                    if s_loc >= 0:
                        d = jnp.where(sub8 >= s_loc, d, NEG_BIG)
                    xval = jnp.exp2(d) * qt[half]
                    col = jnp.sum(xval, axis=-1, keepdims=True)
                    acc[half] = jnp.where(lane8 == base + s, col, acc[half])
            pieces += acc
        a_diag = jnp.concatenate(pieces, axis=0)
        a = (a_off + a_diag[:, :C]).astype(BF16)

        st = st_ref[hh]
        o = jnp.dot(a, iv, preferred_element_type=F32) + _dot_nt(qtil.astype(BF16), st.astype(BF16))
        st_ref[hh] = st * jnp.exp2(r_last) + _dot_tn(iv, kst.astype(BF16))

        ms = jnp.mean(o * o, axis=-1, keepdims=True)
        o = o * lax.rsqrt(ms + RMS_EPS) * nw
        o_ref[pl.ds(r0, C), sl] = (o * gs).astype(BF16)

    def chunk(ci, carry):
        for u in range(HG_UNROLL):
            r0 = pl.multiple_of((ci * HG_UNROLL + u) * C, C)
            for hh in range(hps):
                chunk_head(r0, hh, u * hps + hh)
        return carry

    lax.fori_loop(0, ts // (C * HG_UNROLL), chunk, 0)


def hgrn2(hq, hf, hi, hg, lb_logits, norm_w, *, layer, batch, seq):
    n, width = hq.shape
    hps = HG_HEADS_PER_STEP
    groups = width // (hps * HG_DK)
    ts = min(HG_TS, seq)
    spb = seq // ts
    depth = lb_logits.shape[0]
    blk = pl.BlockSpec((ts, hps * HG_DK), lambda b, h, t: (b * spb + t, h))
    return pl.pallas_call(
        functools.partial(_hgrn_kernel, layer=layer, ts=ts, hps=hps),
        grid=(batch, groups, spb),
        in_specs=[blk, blk, blk, blk,
                  pl.BlockSpec((depth, hps * HG_DK), lambda b, h, t: (0, h)),
                  pl.BlockSpec((1, HG_DK), lambda b, h, t: (0, 0))],
        out_specs=blk,
        out_shape=jax.ShapeDtypeStruct((n, width), BF16),
        scratch_shapes=[pltpu.VMEM((hps, HG_DK, HG_DK), F32), pltpu.VMEM((HG_UNROLL * hps, HG_CHUNK, HG_DK), F32)],
        compiler_params=_cparams(("parallel", "parallel", "arbitrary")),
        name="hgrn2",
    )(hq, hf, hi, hg, lb_logits, norm_w.reshape(1, HG_DK))


def _fox_kernel(first_ref, q_ref, k_ref, v_ref, nf_ref, o_ref, vaug_ref, acc_ref, m_ref, za_ref, zb_ref,
                *, t, tk, seq):
    nq = seq // t
    sub = t // tk
    pair_id = pl.program_id(0) * pl.num_programs(1) + pl.program_id(1)

    vaug_ref[:, :LANES] = v_ref[...]
    vaug_ref[:, LANES:] = jnp.ones((seq, LANES), BF16)

    lane = lax.broadcasted_iota(jnp.int32, (t, LANES), 1)
    row = lax.broadcasted_iota(jnp.int32, (t, tk), 0)
    col = lax.broadcasted_iota(jnp.int32, (t, tk), 1)

    def q_block(qi, carry):
        r0 = pl.multiple_of(qi * t, t)
        q = q_ref[pl.ds(r0, t), :]
        zero = jnp.zeros_like(q)
        qh = [jnp.where(lane < FOX_DH, q, zero), jnp.where(lane >= FOX_DH, q, zero)]
        for h in range(2):
            acc_ref[h] = jnp.zeros((t, 2 * LANES), F32)
            m_ref[h] = jnp.full((t, LANES), -jnp.inf, F32)

        def logits(j, z_ref):
            c0 = pl.multiple_of(j * tk, tk)
            kb = k_ref[pl.ds(c0, tk), :]
            for h in range(2):
                z_ref[h] = _dot_nt(qh[h], kb) + nf_ref[0, h, :, pl.ds(c0, tk)]

        def consume(j, z_ref, diag):
            c0 = pl.multiple_of(j * tk, tk)
            vb = vaug_ref[pl.ds(c0, tk), :]
            for h in range(2):
                z = z_ref[h]
                if diag is not None:
                    z = jnp.where(col + diag * tk <= row, z, MASK_VALUE)
                m_old = m_ref[h]
                m_new = jnp.maximum(m_old, jnp.max(z, axis=-1, keepdims=True))
                alpha = jnp.exp2(m_old - m_new)
                p = jnp.exp2((z - jnp.concatenate([m_new] * (tk // LANES), axis=1)).astype(BF16))
                pv = jnp.dot(p, vb, preferred_element_type=F32)
                acc_ref[h] = jnp.concatenate([alpha, alpha], axis=1) * acc_ref[h] + pv
                m_ref[h] = m_new

        d0 = qi * sub
        start = jnp.minimum(first_ref[pair_id * nq + qi], d0)
        before = d0 - start
        logits(start, za_ref)

        def pair(p, c):
            j = start + 2 * p
            logits(j + 1, zb_ref)
            consume(j, za_ref, None)
            logits(j + 2, za_ref)
            consume(j + 1, zb_ref, None)
            return c

        lax.fori_loop(0, before // 2, pair, 0)

        def diagonal(cur, nxt):
            for s in range(sub):
                if s + 1 < sub:
                    logits(d0 + s + 1, nxt)
                consume(d0 + s, cur, s)
                cur, nxt = nxt, cur

        @pl.when(before % 2 == 1)
        def _():
            logits(d0, zb_ref)
            consume(d0 - 1, za_ref, None)
            diagonal(zb_ref, za_ref)

        @pl.when(before % 2 == 0)
        def _():
            diagonal(za_ref, zb_ref)

        a0 = acc_ref[0]
        a1 = acc_ref[1]
        o0 = a0[:, :LANES] / a0[:, LANES:]
        o1 = a1[:, :LANES] / a1[:, LANES:]
        o_ref[pl.ds(r0, t), :] = jnp.where(lane < FOX_DH, o0, o1).astype(BF16)
        return carry

    lax.fori_loop(0, nq, q_block, 0)


def fox_attn(first, aq, ak, av, negf, *, batch, seq):
    n, width = aq.shape
    pairs = width // LANES
    t = min(FOX_T, seq)
    tk = min(FOX_TK, t)
    whole = pl.BlockSpec((seq, LANES), lambda b, p, f: (b, p))
    grid_spec = pltpu.PrefetchScalarGridSpec(
        num_scalar_prefetch=1,
        grid=(batch, pairs),
        in_specs=[whole, whole, whole, pl.BlockSpec((1, 2, 1, seq), lambda b, p, f: (b, p, 0, 0))],
        out_specs=whole,
        scratch_shapes=[pltpu.VMEM((seq, 2 * LANES), BF16), pltpu.VMEM((2, t, 2 * LANES), F32),
                        pltpu.VMEM((2, t, LANES), F32), pltpu.VMEM((2, t, tk), F32), pltpu.VMEM((2, t, tk), F32)],
    )
    return pl.pallas_call(
        functools.partial(_fox_kernel, t=t, tk=tk, seq=seq),
        grid_spec=grid_spec,
        out_shape=jax.ShapeDtypeStruct((n, width), BF16),
        compiler_params=_cparams(("parallel", "parallel")),
        name="fox_attn",
    )(first, aq, ak, av, negf)


def _mixer_out_kernel(oa_ref, ob_ref, x_ref, mod_ref, wg_ref, wbr_ref, wo_ref, ln_ref, wr_ref, br_ref,
                      x1_ref, h2_ref, lg_ref, *, d, alpha):
    shift1, scale1, gate1 = mod_ref[0, 0:1, :], mod_ref[0, 1:2, :], mod_ref[0, 2:3, :]
    shift2, scale2 = mod_ref[0, 3:4, :], mod_ref[0, 4:5, :]
    x = x_ref[...]
    hb = (x * (1.0 + scale1) + shift1).astype(BF16)
    ga = jnp.dot(hb, wg_ref[:, :d], preferred_element_type=F32)
    gb = jnp.dot(hb, wg_ref[:, d:], preferred_element_type=F32)
    pa = jnp.dot(oa_ref[...], wbr_ref[:d, :], preferred_element_type=F32)
    pb = jnp.dot(ob_ref[...], wbr_ref[d:, :], preferred_element_type=F32)
    merged = jax.nn.sigmoid(ga) * pa + jax.nn.sigmoid(gb) * pb
    y = jnp.dot(merged.astype(BF16), wo_ref[...], preferred_element_type=F32)
    x1 = _layer_norm(alpha * x + (1.0 + gate1) * y, ln_ref[0:1, :], ln_ref[1:2, :])
    x1_ref[...] = x1
    h2 = x1 * (1.0 + scale2) + shift2
    h2b = h2.astype(BF16)
    h2_ref[...] = h2b
    lg_ref[...] = _dot_split(h2, h2b, wr_ref) + br_ref[...]


def mixer_out(oa, ob, x2, mod, wg, wbr, wo, ln, wr, br, *, seq, alpha):
    n, d = x2.shape
    tpb = seq // TM
    row = lambda i: (i, 0)
    const = lambda shape: pl.BlockSpec(shape, lambda i: (0, 0), pipeline_mode=pl.Buffered(1))
    return pl.pallas_call(
        functools.partial(_mixer_out_kernel, d=d, alpha=alpha),
        grid=(n // TM,),
        in_specs=[pl.BlockSpec((TM, d), row), pl.BlockSpec((TM, d), row), pl.BlockSpec((TM, d), row),
                  pl.BlockSpec((1, 6, d), lambda i: (i // tpb, 0, 0)),
                  const((d, 2 * d)), const((2 * d, d)), const((d, d)), const((2, d)),
                  pl.BlockSpec((2, d, LANES), lambda i: (0, 0, 0), pipeline_mode=pl.Buffered(1)),
                  const((1, LANES))],
        out_specs=[pl.BlockSpec((TM, d), row), pl.BlockSpec((TM, d), row), pl.BlockSpec((TM, LANES), row)],
        out_shape=[jax.ShapeDtypeStruct((n, d), F32), jax.ShapeDtypeStruct((n, d), BF16),
                   jax.ShapeDtypeStruct((n, LANES), F32)],
        compiler_params=_cparams(("parallel",)),
        name="mixer_out",
    )(oa, ob, x2, mod, wg, wbr, wo, ln, wr, br)


def _tile_rows(t):
    rows = t * TOP_K + N_EXPERTS * (CHUNK - 1)
    return -(-rows // (8 * CHUNK)) * (8 * CHUNK)


def _dispatch_kernel(lg_ref, h2_ref, xs_ref, ys0_ref, rw_ref, cnt_ref, *, t, rows, nt):
    ys0_ref[...] = jnp.zeros_like(ys0_ref)

    @pl.when(pl.program_id(0) < nt)
    def _():
        _route_tile(lg_ref, h2_ref, xs_ref, rw_ref, cnt_ref, t=t, rows=rows)


def _route_tile(lg_ref, h2_ref, xs_ref, rw_ref, cnt_ref, *, t, rows):
    lane = lax.broadcasted_iota(jnp.int32, (t, LANES), 1).astype(F32)
    lg = lg_ref[...]
    onehots, vals = [], []
    for _ in range(TOP_K):
        mx = jnp.max(lg, axis=-1, keepdims=True)
        idx = jnp.min(jnp.where(lg == mx, lane, float(LANES)), axis=-1, keepdims=True)
        oh = lane == idx
        onehots.append(oh)
        vals.append(mx)
        lg = jnp.where(oh, -jnp.inf, lg)
    es = [jnp.exp(v - vals[0]) for v in vals]
    den = functools.reduce(lambda a, b: a + b, es)
    ws = [e / den for e in es]

    member = functools.reduce(lambda a, b: a + b, [jnp.where(oh, 1.0, 0.0) for oh in onehots])
    rr = lax.broadcasted_iota(jnp.int32, (t, t), 0)
    cc = lax.broadcasted_iota(jnp.int32, (t, t), 1)
    strict_lower = jnp.where(cc < rr, 1.0, 0.0).astype(BF16)
    rank = jnp.dot(strict_lower, member.astype(BF16), preferred_element_type=F32)
    cnt = jnp.sum(member, axis=0, keepdims=True)
    nch = jnp.floor((cnt + (CHUNK - 1)) * (1.0 / CHUNK))
    ur = lax.broadcasted_iota(jnp.int32, (LANES, LANES), 0)
    uc = lax.broadcasted_iota(jnp.int32, (LANES, LANES), 1)
    strict_upper = jnp.where(ur < uc, 1.0, 0.0).astype(BF16)
    toff = jnp.dot(jnp.broadcast_to(nch, (8, LANES)).astype(BF16), strict_upper,
                   preferred_element_type=F32)[0:1, :]
    pos = toff * float(CHUNK) + rank

    rw = jnp.full((t, LANES), -1.0, F32)
    for kk in range(TOP_K):
        r_k = jnp.sum(jnp.where(onehots[kk], pos, 0.0), axis=-1, keepdims=True)
        rw = jnp.where(lane == kk, r_k, rw)
        rw = jnp.where(lane == TOP_K + kk, ws[kk], rw)
    rw_ref[...] = rw
    cnt_ref[0] = jnp.broadcast_to(cnt, (8, LANES))

    rt = rw.T
    h2 = h2_ref[...]
    for r0 in range(0, rows, PERM_STRIP):
        riota = (lax.broadcasted_iota(jnp.int32, (PERM_STRIP, t), 0) + r0).astype(F32)
        pm = jnp.zeros((PERM_STRIP, t), F32)
        for kk in range(TOP_K):
            pm = jnp.where(riota == rt[kk:kk + 1, :], 1.0, pm)
        xs_ref[r0:r0 + PERM_STRIP, :] = jnp.dot(pm.astype(BF16), h2, preferred_element_type=F32).astype(BF16)


def dispatch(logits, h2):
    n, d = h2.shape
    t = min(ROUTE_T, n)
    rows = _tile_rows(t)
    nt = n // t
    assert 2 * BLOCK_CHUNKS * CHUNK <= rows
    tile = lambda i: (jnp.minimum(i, nt - 1), 0)
    return pl.pallas_call(
        functools.partial(_dispatch_kernel, t=t, rows=rows, nt=nt),
        grid=(nt + 1,),
        in_specs=[pl.BlockSpec((t, LANES), tile), pl.BlockSpec((t, d), tile)],
        out_specs=[pl.BlockSpec((rows, d), tile), pl.BlockSpec((rows, d), lambda i: (i, 0)),
                   pl.BlockSpec((t, LANES), tile),
                   pl.BlockSpec((1, 8, LANES), lambda i: (jnp.minimum(i, nt - 1), 0, 0))],
        out_shape=[jax.ShapeDtypeStruct((nt * rows, d), BF16),
                   jax.ShapeDtypeStruct(((nt + 1) * rows, d), BF16),
                   jax.ShapeDtypeStruct((n, LANES), F32), jax.ShapeDtypeStruct((nt, 8, LANES), F32)],
        compiler_params=_cparams(("arbitrary",)),
        name="dispatch",
    )(logits, h2)


def _num_blocks(nt, t):
    max_chunks = nt * ((t * TOP_K + N_EXPERTS * (CHUNK - 1)) // CHUNK)
    return -(-max_chunks // BLOCK_CHUNKS) + N_EXPERTS


def _routing_tables(cnt, t):
    nt = cnt.shape[0]
    e = N_EXPERTS
    cpt = _tile_rows(t) // CHUNK
    nb = _num_blocks(nt, t)
    nch = (cnt + (CHUNK - 1)) // CHUNK
    toff = jnp.cumsum(nch, axis=1) - nch
    n_e = jnp.sum(nch, axis=0)
    nblk = (n_e + (BLOCK_CHUNKS - 1)) // BLOCK_CHUNKS
    bend = jnp.cumsum(nblk)
    bstart = bend - nblk
    total_blocks = bend[-1]
    cumj_incl = jnp.cumsum(nch, axis=0)
    cumj = cumj_incl - nch

    b = jnp.arange(nb, dtype=jnp.int32)
    last = jnp.maximum(total_blocks - 1, 0)
    bex = jnp.minimum(jnp.sum(bend[None, :] <= jnp.minimum(b, last)[:, None], axis=1), e - 1).astype(jnp.int32)
    bvalid = b < total_blocks
    sel = (bex[:, None] == jnp.arange(e, dtype=jnp.int32)[None, :]).astype(jnp.int32)
    row = lambda tab: jnp.sum(sel[:, :, None] * tab.T[None, :, :], axis=1)
    cum_b, excl_b, toff_b = row(cumj_incl), row(cumj), row(toff)
    n_b = jnp.sum(sel * n_e[None, :], axis=1)
    s0 = (b - jnp.sum(sel * bstart[None, :], axis=1)) * BLOCK_CHUNKS
    s = s0[:, None] + jnp.arange(BLOCK_CHUNKS, dtype=jnp.int32)[None, :]
    valid = bvalid[:, None] & (s < n_b[:, None])
    j = jnp.minimum(jnp.sum(cum_b[:, None, :] <= s[:, :, None], axis=2), nt - 1)
    jsel = (j[:, :, None] == jnp.arange(nt, dtype=jnp.int32)[None, None, :]).astype(jnp.int32)
    pick = lambda tab_b: jnp.sum(jsel * tab_b[:, None, :], axis=2)
    chunk = j * cpt + pick(toff_b) + (s - pick(excl_b))
    src = jnp.where(valid, chunk, 0).astype(jnp.int32).reshape(-1)
    spare = nt * cpt + (b[:, None] % 2) * BLOCK_CHUNKS + jnp.arange(BLOCK_CHUNKS, dtype=jnp.int32)[None, :]
    dst = jnp.where(valid, chunk, spare).astype(jnp.int32).reshape(-1)
    return bex, bvalid.astype(jnp.int32), src, dst, nb


def _expert_kernel(bex_ref, bvalid_ref, src_ref, dst_ref, *refs, nb):
    xin = refs[:BLOCK_CHUNKS]
    wg_ref, bg_ref, wu_ref, bu_ref, wd_ref, bd_ref, ys_in_ref, ys_ref, xbuf, ybuf, wbf, sem = refs[BLOCK_CHUNKS:]
    del ys_in_ref
    i = pl.program_id(0)
    slot = i % 2

    def out_copy(step, sl, c):
        row = pl.multiple_of(dst_ref[step * BLOCK_CHUNKS + c] * CHUNK, CHUNK)
        return pltpu.make_async_copy(ybuf.at[sl, pl.ds(c * CHUNK, CHUNK), :],
                                     ys_ref.at[pl.ds(row, CHUNK), :], sem.at[sl])

    def wait_step(step, sl):
        @pl.when(bvalid_ref[step] == 1)
        def _():
            for c in range(BLOCK_CHUNKS):
                out_copy(step, sl, c).wait()

    @pl.when(i >= 2)
    def _():
        wait_step(jnp.maximum(i - 2, 0), slot)

    prev = jnp.maximum(i - 1, 0)
    new_expert = (i == 0) | (bex_ref[i] != bex_ref[prev])

    @pl.when((bvalid_ref[i] == 1) & new_expert)
    def _():
        wbf[0] = wg_ref[0, 0].astype(BF16)
        wbf[1] = wu_ref[0, 0].astype(BF16)
        wbf[2] = wd_ref[0, 0].astype(BF16)

    @pl.when(bvalid_ref[i] == 1)
    def _():
        for c in range(BLOCK_CHUNKS):
            xbuf[c * CHUNK:(c + 1) * CHUNK, :] = xin[c][...]
        xb = xbuf[...]
        g = jnp.minimum(jnp.dot(xb, wbf[0], preferred_element_type=F32) + bg_ref[0, 0], SWIGLU_LIMIT)
        u = jnp.clip(jnp.dot(xb, wbf[1], preferred_element_type=F32) + bu_ref[0, 0], -SWIGLU_LIMIT, SWIGLU_LIMIT)
        a = (u + 1.0) * (g * jax.nn.sigmoid(SWIGLU_ALPHA * g))
        y = jnp.dot(a.astype(BF16), wbf[2], preferred_element_type=F32) + bd_ref[0, 0]
        ybuf[slot] = y.astype(BF16)
        for c in range(BLOCK_CHUNKS):
            out_copy(i, slot, c).start()

    @pl.when(i == nb - 1)
    def _():
        @pl.when(i >= 1)
        def _():
            wait_step(jnp.maximum(i - 1, 0), 1 - slot)
        wait_step(i, slot)


def experts(xs, ys_init, bex, bvalid, src, dst, w_gate, b_gate, w_up, b_up, w_down, b_down, *, nb, layer):
    d = xs.shape[1]
    depth, e, _, de = w_gate.shape
    rows = BLOCK_CHUNKS * CHUNK

    def chunk_spec(c):
        return pl.BlockSpec((CHUNK, d), lambda i, be, bv, sc, dc: (sc[i * BLOCK_CHUNKS + c], 0))

    wspec = lambda shape: pl.BlockSpec((1,) + shape, lambda i, be, bv, sc, dc: (layer, be[i], 0, 0))
    grid_spec = pltpu.PrefetchScalarGridSpec(
        num_scalar_prefetch=4,
        grid=(nb,),
        in_specs=[chunk_spec(c) for c in range(BLOCK_CHUNKS)] + [
            wspec((1, d, de)), wspec((1, 1, de)), wspec((1, d, de)), wspec((1, 1, de)),
            wspec((1, de, d)), wspec((1, 1, d)), pl.BlockSpec(memory_space=pl.ANY)],
        out_specs=pl.BlockSpec(memory_space=pl.ANY),
        scratch_shapes=[pltpu.VMEM((rows, d), BF16), pltpu.VMEM((2, rows, d), BF16),
                        pltpu.VMEM((3, d, de), BF16), pltpu.SemaphoreType.DMA((2,))],
    )
    n_in = 4 + BLOCK_CHUNKS + 6
    return pl.pallas_call(
        functools.partial(_expert_kernel, nb=nb),
        grid_spec=grid_spec,
        out_shape=jax.ShapeDtypeStruct(ys_init.shape, BF16),
        input_output_aliases={n_in: 0},
        compiler_params=_cparams(("arbitrary",)),
        name="experts",
    )(bex, bvalid, src, dst, *([xs] * BLOCK_CHUNKS),
      w_gate, b_gate.reshape(depth, e, 1, de), w_up, b_up.reshape(depth, e, 1, de),
      w_down, b_down.reshape(depth, e, 1, d), ys_init)


def _combine_kernel(ys_ref, rw_ref, x1_ref, mod_ref, ln_ref, o_ref, *, t, rows, alpha):
    gate2 = mod_ref[0, 5:6, :]
    rw = rw_ref[...]
    y2 = jnp.zeros((t, x1_ref.shape[1]), F32)
    for c0 in range(0, rows, PERM_STRIP):
        liota = (lax.broadcasted_iota(jnp.int32, (t, PERM_STRIP), 1) + c0).astype(F32)
        wm = jnp.zeros((t, PERM_STRIP), F32)
        for kk in range(TOP_K):
            wm = jnp.where(liota == rw[:, kk:kk + 1], rw[:, TOP_K + kk:TOP_K + kk + 1], wm)
        y2 = y2 + jnp.dot(wm.astype(BF16), ys_ref[c0:c0 + PERM_STRIP, :], preferred_element_type=F32)
    o_ref[...] = _layer_norm(alpha * x1_ref[...] + (1.0 + gate2) * y2, ln_ref[0:1, :], ln_ref[1:2, :])


def combine(ys, rw, x1, mod, ln, *, seq, alpha):
    n, d = x1.shape
    t = min(ROUTE_T, n)
    rows = _tile_rows(t)
    tpb = max(seq // t, 1)
    return pl.pallas_call(
        functools.partial(_combine_kernel, t=t, rows=rows, alpha=alpha),
        grid=(n // t,),
        in_specs=[pl.BlockSpec((rows, d), lambda i: (i, 0)),
                  pl.BlockSpec((t, LANES), lambda i: (i, 0)),
                  pl.BlockSpec((t, d), lambda i: (i, 0)),
                  pl.BlockSpec((1, 6, d), lambda i: (i // tpb, 0, 0)),
                  pl.BlockSpec((2, d), lambda i: (0, 0))],
        out_specs=pl.BlockSpec((t, d), lambda i: (i, 0)),
        out_shape=jax.ShapeDtypeStruct((n, d), F32),
        compiler_params=_cparams(("parallel",)),
        name="combine",
    )(ys, rw, x1, mod, ln)


def moe_grouped(logits, h2, w_gate, b_gate, w_up, b_up, w_down, b_down, *, layer):
    n, d = h2.shape
    t = min(ROUTE_T, n)
    xs, ys_init, rw, cnt = dispatch(logits, h2)
    cnt_i = cnt[:, 0, :N_EXPERTS].astype(jnp.int32)
    bex, bvalid, src, dst, nb = _routing_tables(cnt_i, t)
    ys = experts(xs, ys_init, bex, bvalid, src, dst, w_gate, b_gate, w_up, b_up, w_down, b_down,
                 nb=nb, layer=layer)
    return ys, rw


def kernel(x, c, w_in, fox_f_bias, hg_lb_logits, hg_norm_w, w_branch, w_out, ada_w, ada_b,
           ln1_g, ln1_b, w_router, b_router, w_gate, b_gate, w_up, b_up, w_down, b_down,
           ln2_g, ln2_b):
    batch, seq, d = x.shape
    depth = w_in.shape[0]
    n = batch * seq
    alpha = (2 * depth) ** 0.25
    hgw = HG_HEADS * HG_DK
    foxw = FOX_HEADS * FOX_DH
    off_af = 4 * hgw + 3 * foxw
    off_g = off_af + FOX_HEADS

    mod_all = ada_mod(c, ada_w, ada_b).reshape(depth, batch, 6, d)
    x2 = x.reshape(n, d)

    for l in range(depth):
        mod = mod_all[l]
        w7 = w_in[l, :, :off_af].astype(BF16)
        waf = _split_hi_lo(jnp.pad(w_in[l, :, off_af:off_g], ((0, 0), (0, LANES - FOX_HEADS))))
        wg = w_in[l, :, off_g:].astype(BF16)
        hq, hf, hi, hg, aq, ak, av, af, qq, kk, qk = in_proj(x2, mod, w7, waf, seq)

        negf, stats = fox_decay(af, fox_f_bias[l], qq, kk, qk, batch=batch, seq=seq, heads=FOX_HEADS)
        first = fox_first_blocks(stats, heads=FOX_HEADS, sub=min(FOX_T, seq) // min(FOX_TK, seq))
        o_a = hgrn2(hq, hf, hi, hg, hg_lb_logits, hg_norm_w[l], layer=l, batch=batch, seq=seq)
        o_b = fox_attn(first, aq, ak, av, negf, batch=batch, seq=seq)

        wr = _split_hi_lo(jnp.pad(w_router[l], ((0, 0), (0, LANES - N_EXPERTS))))
        br = jnp.pad(b_router[l], (0, LANES - N_EXPERTS), constant_values=NEG_BIG).reshape(1, LANES)
        x1, h2, logits = mixer_out(
            o_a, o_b, x2, mod, wg, w_branch[l].astype(BF16), w_out[l].astype(BF16),
            jnp.stack([ln1_g[l], ln1_b[l]]), wr, br, seq=seq, alpha=alpha)

        ys, rw = moe_grouped(logits, h2, w_gate, b_gate, w_up, b_up, w_down, b_down, layer=l)
        x2 = combine(ys, rw, x1, mod, jnp.stack([ln2_g[l], ln2_b[l]]), seq=seq, alpha=alpha)
    return x2.reshape(batch, seq, d)
```

```python
import functools
import math

import jax
import jax.numpy as jnp
from jax import lax
from jax.experimental import pallas as pl
from jax.experimental.pallas import tpu as pltpu

F32 = jnp.float32
BF16 = jnp.bfloat16
HIGHEST = lax.Precision.HIGHEST

HG_HEADS = 8
HG_DK = 128
FOX_HEADS = 16
FOX_DH = 64
N_EXPERTS = 32
TOP_K = 4
LOG_FLOOR = 1e-30
MASK_VALUE = -1e30
SWIGLU_LIMIT = 7.0
SWIGLU_ALPHA = 1.702
LN_EPS = 1e-5
RMS_EPS = 1e-6
LOG2E = math.log2(math.e)

LANES = 128
BF16_SUBLANES = 16
VMEM_LIMIT = 56 * 1024 * 1024

HG_CHUNK = 64
HG_SUB = 16
HG_TS = 1024
HG_UNROLL = 4
HG_HEADS_PER_STEP = 8
FOX_T = 512
FOX_TK = 512
TM = 512
ROUTE_T = 512
CHUNK = BF16_SUBLANES
BLOCK_CHUNKS = 32
PERM_STRIP = 512
NEG_BIG = -1e30


def _cparams(sem):
    return pltpu.CompilerParams(dimension_semantics=sem, vmem_limit_bytes=VMEM_LIMIT)


def _silu(v):
    return v * jax.nn.sigmoid(v)


def _split3(v):
    a = v.astype(BF16)
    r = v - a.astype(F32)
    b = r.astype(BF16)
    c = (r - b.astype(F32)).astype(BF16)
    return a, b, c


def _dot01(m01, v):
    a, b, c = _split3(v)
    d = lambda t: jnp.dot(m01, t, preferred_element_type=F32)
    return d(a) + d(b) + d(c)


def _dot_split(v, v_hi, w_ref):
    v_lo = (v - v_hi.astype(F32)).astype(BF16)
    d = lambda a, b: jnp.dot(a, b, preferred_element_type=F32)
    return d(v_hi, w_ref[0]) + (d(v_hi, w_ref[1]) + d(v_lo, w_ref[0]))


def _split_hi_lo(w):
    hi = w.astype(BF16)
    return jnp.stack([hi, (w - hi.astype(F32)).astype(BF16)])


def _dot_nt(a, b):
    return lax.dot_general(a, b, (((1,), (1,)), ((), ())), preferred_element_type=F32)


def _dot_tn(a, b):
    return lax.dot_general(a, b, (((0,), (0,)), ((), ())), preferred_element_type=F32)


def _layer_norm(z, g, b):
    mu = jnp.mean(z, axis=-1, keepdims=True)
    zc = z - mu
    var = jnp.mean(zc * zc, axis=-1, keepdims=True)
    return zc * lax.rsqrt(var + LN_EPS) * g + b


def _mod_kernel(c_ref, w_ref, b_ref, o_ref):
    ca = _silu(c_ref[...])
    o_ref[0] = jnp.dot(ca, w_ref[0], precision=HIGHEST, preferred_element_type=F32) + b_ref[0]


def ada_mod(c, ada_w, ada_b):
    depth, d, d6 = ada_w.shape
    b = c.shape[0]
    tn = min(d6, 1536)
    return pl.pallas_call(
        _mod_kernel,
        grid=(depth, d6 // tn),
        in_specs=[pl.BlockSpec((b, d), lambda l, j: (0, 0)),
                  pl.BlockSpec((1, d, tn), lambda l, j: (l, 0, j)),
                  pl.BlockSpec((1, 1, tn), lambda l, j: (l, 0, j))],
        out_specs=pl.BlockSpec((1, b, tn), lambda l, j: (l, 0, j)),
        out_shape=jax.ShapeDtypeStruct((depth, b, d6), F32),
        compiler_params=_cparams(("parallel", "parallel")),
        name="ada_mod",
    )(c, ada_w, ada_b.reshape(depth, 1, d6))


def _hgrn_lower_bound(lg, layer):
    rows = [lg[i:i + 1, :] for i in range(lg.shape[0])]
    mx = functools.reduce(jnp.maximum, rows)
    ex = [jnp.exp(rw - mx) for rw in rows]
    den = functools.reduce(lambda a, b: a + b, ex)
    lb = jnp.zeros_like(mx)
    for i in range(1, layer + 1):
        lb = lb + ex[i] / den
    return jnp.clip(lb, 0.0, 1.0)


def _hgrn_consts():
    rr = lax.broadcasted_iota(jnp.int32, (HG_CHUNK, HG_CHUNK), 0)
    cc = lax.broadcasted_iota(jnp.int32, (HG_CHUNK, HG_CHUNK), 1)
    tril = jnp.where(cc <= rr, 1.0, 0.0).astype(BF16)
    sub8 = lax.broadcasted_iota(jnp.int32, (8, LANES), 0)
    lane8 = lax.broadcasted_iota(jnp.int32, (8, LANES), 1)
    return tril, sub8, lane8


def _hgrn_chunk(qs, sig, iv, gs, lb, nw, st_ref, c_ref, consts):
    C, SUB = HG_CHUNK, HG_SUB
    nsub = C // SUB
    tril, sub8, lane8 = consts
    one_m_lb = 1.0 - lb
    f = lb + one_m_lb * sig
    g = jnp.log2(jnp.maximum(f, LOG_FLOOR))
    k = one_m_lb * (1.0 - sig)
    b = _dot01(tril, g)
    r_end = [b[(j + 1) * SUB - 1:(j + 1) * SUB, :] for j in range(nsub)]
    bc = lambda v: jnp.broadcast_to(v, (SUB, LANES))
    r_prev = jnp.concatenate([jnp.zeros((SUB, LANES), F32)] + [bc(r_end[j]) for j in range(nsub - 1)], axis=0)
    r_own = jnp.concatenate([bc(r_end[j]) for j in range(nsub)], axis=0)
    r_last = r_end[nsub - 1]
    e_prev = jnp.concatenate([jnp.ones((SUB, LANES), F32)]
                             + [bc(jnp.exp2(r_end[j])) for j in range(nsub - 1)], axis=0)
    e_tail = jnp.concatenate([bc(jnp.exp2(r_last - r_end[j])) for j in range(nsub)], axis=0)
    qhat = qs * jnp.exp2(b - r_prev)
    khat = k * jnp.exp2(r_own - b)
    qtil = qhat * e_prev
    kst = khat * e_tail

    lhs, rhs = [], []
    for j in range(nsub - 1):
        lo = (j + 1) * SUB
        qj = qs[lo:, :] * jnp.exp2(b[lo:, :] - r_end[j])
        lhs.append(jnp.concatenate([jnp.zeros((lo, LANES), F32), qj], axis=0))
        parts = []
        if j > 0:
            parts.append(jnp.zeros((j * SUB, LANES), F32))
        parts.append(khat[j * SUB:(j + 1) * SUB, :])
        parts.append(jnp.zeros((C - (j + 1) * SUB, LANES), F32))
        rhs.append(jnp.concatenate(parts, axis=0))
    a_off = _dot_nt(jnp.concatenate(lhs, axis=1).astype(BF16),
                    jnp.concatenate(rhs, axis=1).astype(BF16))

    c_ref[...] = b - jnp.log2(k)
    pieces = []
    for blk in range(nsub):
        base = blk * SUB
        bt = [b[base:base + 8, :], b[base + 8:base + 16, :]]
        qt = [qs[base:base + 8, :], qs[base + 8:base + 16, :]]
        acc = [jnp.zeros((8, LANES), F32), jnp.zeros((8, LANES), F32)]
        for s in range(SUB):
            cs = jnp.broadcast_to(c_ref[base + s:base + s + 1, :], (8, LANES))
            for half in range(2):
                if s >= 8 and half == 0:
                    continue
                d = bt[half] - cs
                s_loc = s - 8 * half
                if s_loc >= 0:
                    d = jnp.where(sub8 >= s_loc, d, NEG_BIG)
                xval = jnp.exp2(d) * qt[half]
                col = jnp.sum(xval, axis=-1, keepdims=True)
                acc[half] = jnp.where(lane8 == base + s, col, acc[half])
        pieces += acc
    a_diag = jnp.concatenate(pieces, axis=0)
    a = (a_off + a_diag[:, :C]).astype(BF16)

    st = st_ref[...]
    o = jnp.dot(a, iv, preferred_element_type=F32) + _dot_nt(qtil.astype(BF16), st.astype(BF16))
    st_ref[...] = st * jnp.exp2(r_last) + _dot_tn(iv, kst.astype(BF16))

    ms = jnp.mean(o * o, axis=-1, keepdims=True)
    return o * lax.rsqrt(ms + RMS_EPS) * nw * gs


def _in_proj_kernel(x_ref, mod_ref, w_ref, waf_ref, sel_ref, hq_ref, hf_ref, hi_ref, hg_ref,
                    aq_ref, ak_ref, av_ref, af_ref, qq_ref, kk_ref, qk_ref, *, d):
    shift = mod_ref[0, 0:1, :]
    scale = mod_ref[0, 1:2, :]
    h = x_ref[...] * (1.0 + scale) + shift
    hb = h.astype(BF16)

    def proj(g):
        return jnp.dot(hb, w_ref[:, g * d:(g + 1) * d], preferred_element_type=F32)

    hq_ref[...] = _silu(proj(0)).astype(BF16)
    hf_ref[...] = jax.nn.sigmoid(proj(1))
    hi_ref[...] = proj(2).astype(BF16)
    hg_ref[...] = _silu(proj(3)).astype(BF16)
    qb = (proj(4) * (FOX_DH ** -0.5 * LOG2E)).astype(BF16)
    kb = proj(5).astype(BF16)
    aq_ref[...] = qb
    ak_ref[...] = kb
    av_ref[...] = proj(6).astype(BF16)
    af_ref[...] = _dot_split(h, hb, waf_ref)
    per_head = lambda v: jnp.dot(v, sel_ref[...], preferred_element_type=F32)
    qq_ref[...] = per_head(qb * qb)
    kk_ref[...] = per_head(kb * kb)
    qk_ref[...] = per_head(qb * kb)


def in_proj(x2, mod, w7, waf, seq):
    n, d = x2.shape
    tpb = seq // TM
    row = lambda i: (i, 0)
    big = lambda dt: jax.ShapeDtypeStruct((n, d), dt)
    small = jax.ShapeDtypeStruct((n, LANES), F32)
    sel = (jnp.arange(d)[:, None] // FOX_DH == jnp.arange(LANES)[None, :]).astype(BF16)
    return pl.pallas_call(
        functools.partial(_in_proj_kernel, d=d),
        grid=(n // TM,),
        in_specs=[pl.BlockSpec((TM, d), row),
                  pl.BlockSpec((1, 6, d), lambda i: (i // tpb, 0, 0)),
                  pl.BlockSpec((d, 7 * d), lambda i: (0, 0), pipeline_mode=pl.Buffered(1)),
                  pl.BlockSpec((2, d, LANES), lambda i: (0, 0, 0), pipeline_mode=pl.Buffered(1)),
                  pl.BlockSpec((d, LANES), lambda i: (0, 0), pipeline_mode=pl.Buffered(1))],
        out_specs=[pl.BlockSpec((TM, d), row)] * 7 + [pl.BlockSpec((TM, LANES), row)] * 4,
        out_shape=[big(BF16), big(F32), big(BF16), big(BF16), big(BF16), big(BF16), big(BF16),
                   small, small, small, small],
        compiler_params=_cparams(("parallel",)),
        name="in_proj",
    )(x2, mod, w7, waf, sel)


def _hgrn_kernel(q_ref, f_ref, i_ref, g_ref, lbl_ref, nw_ref, o_ref, st_ref, c_ref, *, layer, ts, hps):
    C = HG_CHUNK

    @pl.when(pl.program_id(2) == 0)
    def _():
        st_ref[...] = jnp.zeros_like(st_ref)

    lb_all = _hgrn_lower_bound(lbl_ref[...], layer)
    nw = nw_ref[...]
    consts = _hgrn_consts()

    def chunk(ci, carry):
        for u in range(HG_UNROLL):
            r0 = pl.multiple_of((ci * HG_UNROLL + u) * C, C)
            for hh in range(hps):
                sl = slice(hh * HG_DK, (hh + 1) * HG_DK)
                o = _hgrn_chunk(q_ref[pl.ds(r0, C), sl].astype(F32), f_ref[pl.ds(r0, C), sl],
                                i_ref[pl.ds(r0, C), sl], g_ref[pl.ds(r0, C), sl].astype(F32),
                                lb_all[:, sl], nw, st_ref.at[hh], c_ref.at[u * hps + hh], consts)
                o_ref[pl.ds(r0, C), sl] = o.astype(BF16)
        return carry

    lax.fori_loop(0, ts // (C * HG_UNROLL), chunk, 0)


def hgrn2(hq, hf, hi, hg, lb_logits, norm_w, *, layer, batch, seq):
    n, width = hq.shape
    hps = HG_HEADS_PER_STEP
    groups = width // (hps * HG_DK)
    ts = min(HG_TS, seq)
    spb = seq // ts
    depth = lb_logits.shape[0]
    blk = pl.BlockSpec((ts, hps * HG_DK), lambda b, h, t: (b * spb + t, h))
    return pl.pallas_call(
        functools.partial(_hgrn_kernel, layer=layer, ts=ts, hps=hps),
        grid=(batch, groups, spb),
        in_specs=[blk, blk, blk, blk,
                  pl.BlockSpec((depth, hps * HG_DK), lambda b, h, t: (0, h)),
                  pl.BlockSpec((1, HG_DK), lambda b, h, t: (0, 0))],
        out_specs=blk,
        out_shape=jax.ShapeDtypeStruct((n, width), BF16),
        scratch_shapes=[pltpu.VMEM((hps, HG_DK, HG_DK), F32), pltpu.VMEM((HG_UNROLL * hps, HG_CHUNK, HG_DK), F32)],
        compiler_params=_cparams(("parallel", "parallel", "arbitrary")),
        name="hgrn2",
    )(hq, hf, hi, hg, lb_logits, norm_w.reshape(1, HG_DK))


def _fox_decay_kernel(af_ref, bias_ref, qq_ref, kk_ref, qk_ref, nf_ref, st_ref, carry_ref, *, t, tk, heads):
    @pl.when(pl.program_id(1) == 0)
    def _():
        carry_ref[...] = jnp.zeros_like(carry_ref)

    z = af_ref[...] + bias_ref[...]
    ls = jnp.minimum(z, 0.0) - jnp.log(1.0 + jnp.exp(-jnp.abs(z)))
    rr = lax.broadcasted_iota(jnp.int32, (t, t), 0)
    cc = lax.broadcasted_iota(jnp.int32, (t, t), 1)
    tril = jnp.where(cc <= rr, 1.0, 0.0).astype(BF16)
    cs = _dot01(tril, ls) + carry_ref[...]
    carry_ref[...] = cs[t - 1:t, :]
    nf = cs * (-LOG2E)
    nf_ref[0, :, 0, :] = nf.T[:heads, :]

    amax = jnp.max(jnp.sqrt(qq_ref[...]), axis=0, keepdims=True)
    dmin = jnp.min(qk_ref[...] + nf, axis=0, keepdims=True)
    kn = jnp.sqrt(kk_ref[...])
    sub = t // tk
    kmax = [jnp.max(kn[s * tk:(s + 1) * tk, :], axis=0, keepdims=True) for s in range(sub)]
    ends = [nf[(s + 1) * tk - 1:(s + 1) * tk, :] for s in range(sub)]
    pad = jnp.zeros((8 - 2 - 2 * sub, LANES), F32)
    st_ref[0, 0] = jnp.concatenate([amax, dmin] + kmax + ends + [pad], axis=0)


def fox_decay(af, bias, qq, kk, qk, *, batch, seq, heads):
    t = min(FOX_T, seq)
    nblk = seq // t
    row = pl.BlockSpec((t, LANES), lambda b, i: (b * nblk + i, 0))
    return pl.pallas_call(
        functools.partial(_fox_decay_kernel, t=t, tk=min(FOX_TK, t), heads=heads),
        grid=(batch, nblk),
        in_specs=[row, pl.BlockSpec((1, LANES), lambda b, i: (0, 0)), row, row, row],
        out_specs=[pl.BlockSpec((1, heads, 1, t), lambda b, i: (b, 0, 0, i)),
                   pl.BlockSpec((1, 1, 8, LANES), lambda b, i: (b, i, 0, 0))],
        out_shape=[jax.ShapeDtypeStruct((batch, heads, 1, seq), F32),
                   jax.ShapeDtypeStruct((batch, nblk, 8, LANES), F32)],
        scratch_shapes=[pltpu.VMEM((1, LANES), F32)],
        compiler_params=_cparams(("parallel", "arbitrary")),
        name="fox_decay",
    )(af, jnp.pad(bias, (0, LANES - heads)).reshape(1, LANES), qq, kk, qk)


FOX_SKIP_LOG2 = 170.0


def fox_first_blocks(stats, *, heads, sub):
    nb, nblk = stats.shape[0], stats.shape[1]
    amax, dmin = stats[:, :, 0, :heads], stats[:, :, 1, :heads]
    kmax = stats[:, :, 2:2 + sub, :heads].reshape(nb, nblk * sub, heads)
    e_end = stats[:, :, 2 + sub:2 + 2 * sub, :heads].reshape(nb, nblk * sub, heads)
    ub = amax[:, :, None, :] * kmax[:, None, :, :] * 1.02 + 2.0 - dmin[:, :, None, :] + e_end[:, None, :, :]
    earlier = (jnp.arange(nblk) * sub)[None, :, None, None] > jnp.arange(nblk * sub)[None, None, :, None]
    skip = (ub < -FOX_SKIP_LOG2) & earlier
    first = jnp.sum(jnp.cumprod(skip.astype(jnp.int32), axis=2), axis=2)
    first = jnp.min(first.reshape(first.shape[0], nblk, heads // 2, 2), axis=-1)
    return first.transpose(0, 2, 1).reshape(-1).astype(jnp.int32)


def _fox_kernel(first_ref, q_ref, k_ref, v_ref, nf_ref, o_ref, vaug_ref, acc_ref, m_ref, za_ref, zb_ref,
                *, t, tk, seq):
    nq = seq // t
    sub = t // tk
    pair_id = pl.program_id(0) * pl.num_programs(1) + pl.program_id(1)

    vaug_ref[:, :LANES] = v_ref[...]
    vaug_ref[:, LANES:] = jnp.ones((seq, LANES), BF16)

    lane = lax.broadcasted_iota(jnp.int32, (t, LANES), 1)
    row = lax.broadcasted_iota(jnp.int32, (t, tk), 0)
    col = lax.broadcasted_iota(jnp.int32, (t, tk), 1)

    def q_block(qi, carry):
        r0 = pl.multiple_of(qi * t, t)
        q = q_ref[pl.ds(r0, t), :]
        zero = jnp.zeros_like(q)
        qh = [jnp.where(lane < FOX_DH, q, zero), jnp.where(lane >= FOX_DH, q, zero)]
        for h in range(2):
            acc_ref[h] = jnp.zeros((t, 2 * LANES), F32)
            m_ref[h] = jnp.full((t, LANES), -jnp.inf, F32)

        def logits(j, z_ref):
            c0 = pl.multiple_of(j * tk, tk)
            kb = k_ref[pl.ds(c0, tk), :]
            for h in range(2):
                z_ref[h] = _dot_nt(qh[h], kb) + nf_ref[0, h, :, pl.ds(c0, tk)]

        def consume(j, z_ref, diag):
            c0 = pl.multiple_of(j * tk, tk)
            vb = vaug_ref[pl.ds(c0, tk), :]
            for h in range(2):
                z = z_ref[h]
                if diag is not None:
                    z = jnp.where(col + diag * tk <= row, z, MASK_VALUE)
                m_old = m_ref[h]
                m_new = jnp.maximum(m_old, jnp.max(z, axis=-1, keepdims=True))
                alpha = jnp.exp2(m_old - m_new)
                p = jnp.exp2((z - jnp.concatenate([m_new] * (tk // LANES), axis=1)).astype(BF16))
                pv = jnp.dot(p, vb, preferred_element_type=F32)
                acc_ref[h] = jnp.concatenate([alpha, alpha], axis=1) * acc_ref[h] + pv
                m_ref[h] = m_new

        d0 = qi * sub
        start = jnp.minimum(first_ref[pair_id * nq + qi], d0)
        before = d0 - start
        logits(start, za_ref)

        def pair(p, c):
            j = start + 2 * p
            logits(j + 1, zb_ref)
            consume(j, za_ref, None)
            logits(j + 2, za_ref)
            consume(j + 1, zb_ref, None)
            return c

        lax.fori_loop(0, before // 2, pair, 0)

        def diagonal(cur, nxt):
            for s in range(sub):
                if s + 1 < sub:
                    logits(d0 + s + 1, nxt)
                consume(d0 + s, cur, s)
                cur, nxt = nxt, cur

        @pl.when(before % 2 == 1)
        def _():
            logits(d0, zb_ref)
            consume(d0 - 1, za_ref, None)
            diagonal(zb_ref, za_ref)

        @pl.when(before % 2 == 0)
        def _():
            diagonal(za_ref, zb_ref)

        a0 = acc_ref[0]
        a1 = acc_ref[1]
        o0 = a0[:, :LANES] / a0[:, LANES:]
        o1 = a1[:, :LANES] / a1[:, LANES:]
        o_ref[pl.ds(r0, t), :] = jnp.where(lane < FOX_DH, o0, o1).astype(BF16)
        return carry

    lax.fori_loop(0, nq, q_block, 0)


def fox_attn(first, aq, ak, av, negf, *, batch, seq):
    n, width = aq.shape
    pairs = width // LANES
    t = min(FOX_T, seq)
    tk = min(FOX_TK, t)
    whole = pl.BlockSpec((seq, LANES), lambda b, p, f: (b, p))
    grid_spec = pltpu.PrefetchScalarGridSpec(
        num_scalar_prefetch=1,
        grid=(batch, pairs),
        in_specs=[whole, whole, whole, pl.BlockSpec((1, 2, 1, seq), lambda b, p, f: (b, p, 0, 0))],
        out_specs=whole,
        scratch_shapes=[pltpu.VMEM((seq, 2 * LANES), BF16), pltpu.VMEM((2, t, 2 * LANES), F32),
                        pltpu.VMEM((2, t, LANES), F32), pltpu.VMEM((2, t, tk), F32), pltpu.VMEM((2, t, tk), F32)],
    )
    return pl.pallas_call(
        functools.partial(_fox_kernel, t=t, tk=tk, seq=seq),
        grid_spec=grid_spec,
        out_shape=jax.ShapeDtypeStruct((n, width), BF16),
        compiler_params=_cparams(("parallel", "parallel")),
        name="fox_attn",
    )(first, aq, ak, av, negf)


def _mixer_out_kernel(oa_ref, ob_ref, x_ref, mod_ref, wg_ref, wbr_ref, wo_ref, ln_ref, wr_ref, br_ref,
                      x1_ref, h2_ref, lg_ref, *, d, alpha):
    shift1, scale1, gate1 = mod_ref[0, 0:1, :], mod_ref[0, 1:2, :], mod_ref[0, 2:3, :]
    shift2, scale2 = mod_ref[0, 3:4, :], mod_ref[0, 4:5, :]
    x = x_ref[...]
    hb = (x * (1.0 + scale1) + shift1).astype(BF16)
    ga = jnp.dot(hb, wg_ref[:, :d], preferred_element_type=F32)
    gb = jnp.dot(hb, wg_ref[:, d:], preferred_element_type=F32)
    pa = jnp.dot(oa_ref[...], wbr_ref[:d, :], preferred_element_type=F32)
    pb = jnp.dot(ob_ref[...], wbr_ref[d:, :], preferred_element_type=F32)
    merged = jax.nn.sigmoid(ga) * pa + jax.nn.sigmoid(gb) * pb
    y = jnp.dot(merged.astype(BF16), wo_ref[...], preferred_element_type=F32)
    x1 = _layer_norm(alpha * x + (1.0 + gate1) * y, ln_ref[0:1, :], ln_ref[1:2, :])
    x1_ref[...] = x1
    h2 = x1 * (1.0 + scale2) + shift2
    h2b = h2.astype(BF16)
    h2_ref[...] = h2b
    lg_ref[...] = _dot_split(h2, h2b, wr_ref) + br_ref[...]


def mixer_out(oa, ob, x2, mod, wg, wbr, wo, ln, wr, br, *, seq, alpha):
    n, d = x2.shape
    tpb = seq // TM
    row = lambda i: (i, 0)
    const = lambda shape: pl.BlockSpec(shape, lambda i: (0, 0), pipeline_mode=pl.Buffered(1))
    return pl.pallas_call(
        functools.partial(_mixer_out_kernel, d=d, alpha=alpha),
        grid=(n // TM,),
        in_specs=[pl.BlockSpec((TM, d), row), pl.BlockSpec((TM, d), row), pl.BlockSpec((TM, d), row),
                  pl.BlockSpec((1, 6, d), lambda i: (i // tpb, 0, 0)),
                  const((d, 2 * d)), const((2 * d, d)), const((d, d)), const((2, d)),
                  pl.BlockSpec((2, d, LANES), lambda i: (0, 0, 0), pipeline_mode=pl.Buffered(1)),
                  const((1, LANES))],
        out_specs=[pl.BlockSpec((TM, d), row), pl.BlockSpec((TM, d), row), pl.BlockSpec((TM, LANES), row)],
        out_shape=[jax.ShapeDtypeStruct((n, d), F32), jax.ShapeDtypeStruct((n, d), BF16),
                   jax.ShapeDtypeStruct((n, LANES), F32)],
        compiler_params=_cparams(("parallel",)),
        name="mixer_out",
    )(oa, ob, x2, mod, wg, wbr, wo, ln, wr, br)


def _tile_rows(t):
    rows = t * TOP_K + N_EXPERTS * (CHUNK - 1)
    return -(-rows // (8 * CHUNK)) * (8 * CHUNK)


def _dispatch_kernel(lg_ref, h2_ref, xs_ref, ys0_ref, rw_ref, cnt_ref, *, t, rows, nt):
    ys0_ref[...] = jnp.zeros_like(ys0_ref)

    @pl.when(pl.program_id(0) < nt)
    def _():
        _route_tile(lg_ref, h2_ref, xs_ref, rw_ref, cnt_ref, t=t, rows=rows)


def _route_tile(lg_ref, h2_ref, xs_ref, rw_ref, cnt_ref, *, t, rows):
    lane = lax.broadcasted_iota(jnp.int32, (t, LANES), 1).astype(F32)
    lg = lg_ref[...]
    onehots, vals = [], []
    for _ in range(TOP_K):
        mx = jnp.max(lg, axis=-1, keepdims=True)
        idx = jnp.min(jnp.where(lg == mx, lane, float(LANES)), axis=-1, keepdims=True)
        oh = lane == idx
        onehots.append(oh)
        vals.append(mx)
        lg = jnp.where(oh, -jnp.inf, lg)
    es = [jnp.exp(v - vals[0]) for v in vals]
    den = functools.reduce(lambda a, b: a + b, es)
    ws = [e / den for e in es]

    member = functools.reduce(lambda a, b: a + b, [jnp.where(oh, 1.0, 0.0) for oh in onehots])
    rr = lax.broadcasted_iota(jnp.int32, (t, t), 0)
    cc = lax.broadcasted_iota(jnp.int32, (t, t), 1)
    strict_lower = jnp.where(cc < rr, 1.0, 0.0).astype(BF16)
    rank = jnp.dot(strict_lower, member.astype(BF16), preferred_element_type=F32)
    cnt = jnp.sum(member, axis=0, keepdims=True)
    nch = jnp.floor((cnt + (CHUNK - 1)) * (1.0 / CHUNK))
    ur = lax.broadcasted_iota(jnp.int32, (LANES, LANES), 0)
    uc = lax.broadcasted_iota(jnp.int32, (LANES, LANES), 1)
    strict_upper = jnp.where(ur < uc, 1.0, 0.0).astype(BF16)
    toff = jnp.dot(jnp.broadcast_to(nch, (8, LANES)).astype(BF16), strict_upper,
                   preferred_element_type=F32)[0:1, :]
    pos = toff * float(CHUNK) + rank

    rw = jnp.full((t, LANES), -1.0, F32)
    for kk in range(TOP_K):
        r_k = jnp.sum(jnp.where(onehots[kk], pos, 0.0), axis=-1, keepdims=True)
        rw = jnp.where(lane == kk, r_k, rw)
        rw = jnp.where(lane == TOP_K + kk, ws[kk], rw)
    rw_ref[...] = rw
    cnt_ref[0] = jnp.broadcast_to(cnt, (8, LANES))

    rt = rw.T
    h2 = h2_ref[...]
    for r0 in range(0, rows, PERM_STRIP):
        riota = (lax.broadcasted_iota(jnp.int32, (PERM_STRIP, t), 0) + r0).astype(F32)
        pm = jnp.zeros((PERM_STRIP, t), F32)
        for kk in range(TOP_K):
            pm = jnp.where(riota == rt[kk:kk + 1, :], 1.0, pm)
        xs_ref[r0:r0 + PERM_STRIP, :] = jnp.dot(pm.astype(BF16), h2, preferred_element_type=F32).astype(BF16)


def dispatch(logits, h2):
    n, d = h2.shape
    t = min(ROUTE_T, n)
    rows = _tile_rows(t)
    nt = n // t
    assert 2 * BLOCK_CHUNKS * CHUNK <= rows
    tile = lambda i: (jnp.minimum(i, nt - 1), 0)
    return pl.pallas_call(
        functools.partial(_dispatch_kernel, t=t, rows=rows, nt=nt),
        grid=(nt + 1,),
        in_specs=[pl.BlockSpec((t, LANES), tile), pl.BlockSpec((t, d), tile)],
        out_specs=[pl.BlockSpec((rows, d), tile), pl.BlockSpec((rows, d), lambda i: (i, 0)),
                   pl.BlockSpec((t, LANES), tile),
                   pl.BlockSpec((1, 8, LANES), lambda i: (jnp.minimum(i, nt - 1), 0, 0))],
        out_shape=[jax.ShapeDtypeStruct((nt * rows, d), BF16),
                   jax.ShapeDtypeStruct(((nt + 1) * rows, d), BF16),
                   jax.ShapeDtypeStruct((n, LANES), F32), jax.ShapeDtypeStruct((nt, 8, LANES), F32)],
        compiler_params=_cparams(("arbitrary",)),
        name="dispatch",
    )(logits, h2)


def _num_blocks(nt, t):
    max_chunks = nt * ((t * TOP_K + N_EXPERTS * (CHUNK - 1)) // CHUNK)
    return -(-max_chunks // BLOCK_CHUNKS) + N_EXPERTS


def _routing_tables(cnt, t):
    nt = cnt.shape[0]
    e = N_EXPERTS
    cpt = _tile_rows(t) // CHUNK
    nb = _num_blocks(nt, t)
    nch = (cnt + (CHUNK - 1)) // CHUNK
    toff = jnp.cumsum(nch, axis=1) - nch
    n_e = jnp.sum(nch, axis=0)
    nblk = (n_e + (BLOCK_CHUNKS - 1)) // BLOCK_CHUNKS
    bend = jnp.cumsum(nblk)
    bstart = bend - nblk
    total_blocks = bend[-1]
    cumj_incl = jnp.cumsum(nch, axis=0)
    cumj = cumj_incl - nch

    b = jnp.arange(nb, dtype=jnp.int32)
    last = jnp.maximum(total_blocks - 1, 0)
    bex = jnp.minimum(jnp.sum(bend[None, :] <= jnp.minimum(b, last)[:, None], axis=1), e - 1).astype(jnp.int32)
    bvalid = b < total_blocks
    sel = (bex[:, None] == jnp.arange(e, dtype=jnp.int32)[None, :]).astype(jnp.int32)
    row = lambda tab: jnp.sum(sel[:, :, None] * tab.T[None, :, :], axis=1)
    cum_b, excl_b, toff_b = row(cumj_incl), row(cumj), row(toff)
    n_b = jnp.sum(sel * n_e[None, :], axis=1)
    s0 = (b - jnp.sum(sel * bstart[None, :], axis=1)) * BLOCK_CHUNKS
    s = s0[:, None] + jnp.arange(BLOCK_CHUNKS, dtype=jnp.int32)[None, :]
    valid = bvalid[:, None] & (s < n_b[:, None])
    j = jnp.minimum(jnp.sum(cum_b[:, None, :] <= s[:, :, None], axis=2), nt - 1)
    jsel = (j[:, :, None] == jnp.arange(nt, dtype=jnp.int32)[None, None, :]).astype(jnp.int32)
    pick = lambda tab_b: jnp.sum(jsel * tab_b[:, None, :], axis=2)
    chunk = j * cpt + pick(toff_b) + (s - pick(excl_b))
    src = jnp.where(valid, chunk, 0).astype(jnp.int32).reshape(-1)
    spare = nt * cpt + (b[:, None] % 2) * BLOCK_CHUNKS + jnp.arange(BLOCK_CHUNKS, dtype=jnp.int32)[None, :]
    dst = jnp.where(valid, chunk, spare).astype(jnp.int32).reshape(-1)
    return bex, bvalid.astype(jnp.int32), src, dst, nb


def _expert_kernel(bex_ref, bvalid_ref, src_ref, dst_ref, *refs, nb):
    xin = refs[:BLOCK_CHUNKS]
    wg_ref, bg_ref, wu_ref, bu_ref, wd_ref, bd_ref, ys_in_ref, ys_ref, xbuf, ybuf, wbf, sem = refs[BLOCK_CHUNKS:]
    del ys_in_ref
    i = pl.program_id(0)
    slot = i % 2

    def out_copy(step, sl, c):
        row = pl.multiple_of(dst_ref[step * BLOCK_CHUNKS + c] * CHUNK, CHUNK)
        return pltpu.make_async_copy(ybuf.at[sl, pl.ds(c * CHUNK, CHUNK), :],
                                     ys_ref.at[pl.ds(row, CHUNK), :], sem.at[sl])

    def wait_step(step, sl):
        @pl.when(bvalid_ref[step] == 1)
        def _():
            for c in range(BLOCK_CHUNKS):
                out_copy(step, sl, c).wait()

    @pl.when(i >= 2)
    def _():
        wait_step(jnp.maximum(i - 2, 0), slot)

    prev = jnp.maximum(i - 1, 0)
    new_expert = (i == 0) | (bex_ref[i] != bex_ref[prev])

    @pl.when((bvalid_ref[i] == 1) & new_expert)
    def _():
        wbf[0] = wg_ref[0, 0].astype(BF16)
        wbf[1] = wu_ref[0, 0].astype(BF16)
        wbf[2] = wd_ref[0, 0].astype(BF16)

    @pl.when(bvalid_ref[i] == 1)
    def _():
        for c in range(BLOCK_CHUNKS):
            xbuf[c * CHUNK:(c + 1) * CHUNK, :] = xin[c][...]
        xb = xbuf[...]
        g = jnp.minimum(jnp.dot(xb, wbf[0], preferred_element_type=F32) + bg_ref[0, 0], SWIGLU_LIMIT)
        u = jnp.clip(jnp.dot(xb, wbf[1], preferred_element_type=F32) + bu_ref[0, 0], -SWIGLU_LIMIT, SWIGLU_LIMIT)
        a = (u + 1.0) * (g * jax.nn.sigmoid(SWIGLU_ALPHA * g))
        y = jnp.dot(a.astype(BF16), wbf[2], preferred_element_type=F32) + bd_ref[0, 0]
        ybuf[slot] = y.astype(BF16)
        for c in range(BLOCK_CHUNKS):
            out_copy(i, slot, c).start()

    @pl.when(i == nb - 1)
    def _():
        @pl.when(i >= 1)
        def _():
            wait_step(jnp.maximum(i - 1, 0), 1 - slot)
        wait_step(i, slot)


def experts(xs, ys_init, bex, bvalid, src, dst, w_gate, b_gate, w_up, b_up, w_down, b_down, *, nb, layer):
    d = xs.shape[1]
    depth, e, _, de = w_gate.shape
    rows = BLOCK_CHUNKS * CHUNK

    def chunk_spec(c):
        return pl.BlockSpec((CHUNK, d), lambda i, be, bv, sc, dc: (sc[i * BLOCK_CHUNKS + c], 0))

    wspec = lambda shape: pl.BlockSpec((1,) + shape, lambda i, be, bv, sc, dc: (layer, be[i], 0, 0))
    grid_spec = pltpu.PrefetchScalarGridSpec(
        num_scalar_prefetch=4,
        grid=(nb,),
        in_specs=[chunk_spec(c) for c in range(BLOCK_CHUNKS)] + [
            wspec((1, d, de)), wspec((1, 1, de)), wspec((1, d, de)), wspec((1, 1, de)),
            wspec((1, de, d)), wspec((1, 1, d)), pl.BlockSpec(memory_space=pl.ANY)],
        out_specs=pl.BlockSpec(memory_space=pl.ANY),
        scratch_shapes=[pltpu.VMEM((rows, d), BF16), pltpu.VMEM((2, rows, d), BF16),
                        pltpu.VMEM((3, d, de), BF16), pltpu.SemaphoreType.DMA((2,))],
    )
    n_in = 4 + BLOCK_CHUNKS + 6
    return pl.pallas_call(
        functools.partial(_expert_kernel, nb=nb),
        grid_spec=grid_spec,
        out_shape=jax.ShapeDtypeStruct(ys_init.shape, BF16),
        input_output_aliases={n_in: 0},
        compiler_params=_cparams(("arbitrary",)),
        name="experts",
    )(bex, bvalid, src, dst, *([xs] * BLOCK_CHUNKS),
      w_gate, b_gate.reshape(depth, e, 1, de), w_up, b_up.reshape(depth, e, 1, de),
      w_down, b_down.reshape(depth, e, 1, d), ys_init)


def _combine_kernel(ys_ref, rw_ref, x1_ref, mod_ref, ln_ref, o_ref, *, t, rows, alpha):
    gate2 = mod_ref[0, 5:6, :]
    rw = rw_ref[...]
    y2 = jnp.zeros((t, x1_ref.shape[1]), F32)
    for c0 in range(0, rows, PERM_STRIP):
        liota = (lax.broadcasted_iota(jnp.int32, (t, PERM_STRIP), 1) + c0).astype(F32)
        wm = jnp.zeros((t, PERM_STRIP), F32)
        for kk in range(TOP_K):
            wm = jnp.where(liota == rw[:, kk:kk + 1], rw[:, TOP_K + kk:TOP_K + kk + 1], wm)
        y2 = y2 + jnp.dot(wm.astype(BF16), ys_ref[c0:c0 + PERM_STRIP, :], preferred_element_type=F32)
    o_ref[...] = _layer_norm(alpha * x1_ref[...] + (1.0 + gate2) * y2, ln_ref[0:1, :], ln_ref[1:2, :])


def combine(ys, rw, x1, mod, ln, *, seq, alpha):
    n, d = x1.shape
    t = min(ROUTE_T, n)
    rows = _tile_rows(t)
    tpb = max(seq // t, 1)
    return pl.pallas_call(
        functools.partial(_combine_kernel, t=t, rows=rows, alpha=alpha),
        grid=(n // t,),
        in_specs=[pl.BlockSpec((rows, d), lambda i: (i, 0)),
                  pl.BlockSpec((t, LANES), lambda i: (i, 0)),
                  pl.BlockSpec((t, d), lambda i: (i, 0)),
                  pl.BlockSpec((1, 6, d), lambda i: (i // tpb, 0, 0)),
                  pl.BlockSpec((2, d), lambda i: (0, 0))],
        out_specs=pl.BlockSpec((t, d), lambda i: (i, 0)),
        out_shape=jax.ShapeDtypeStruct((n, d), F32),
        compiler_params=_cparams(("parallel",)),
        name="combine",
    )(ys, rw, x1, mod, ln)


def moe_grouped(logits, h2, w_gate, b_gate, w_up, b_up, w_down, b_down, *, layer):
    n, d = h2.shape
    t = min(ROUTE_T, n)
    xs, ys_init, rw, cnt = dispatch(logits, h2)
    cnt_i = cnt[:, 0, :N_EXPERTS].astype(jnp.int32)
    bex, bvalid, src, dst, nb = _routing_tables(cnt_i, t)
    ys = experts(xs, ys_init, bex, bvalid, src, dst, w_gate, b_gate, w_up, b_up, w_down, b_down,
                 nb=nb, layer=layer)
    return ys, rw


def kernel(x, c, w_in, fox_f_bias, hg_lb_logits, hg_norm_w, w_branch, w_out, ada_w, ada_b,
           ln1_g, ln1_b, w_router, b_router, w_gate, b_gate, w_up, b_up, w_down, b_down,
           ln2_g, ln2_b):
    batch, seq, d = x.shape
    depth = w_in.shape[0]
    n = batch * seq
    alpha = (2 * depth) ** 0.25
    hgw = HG_HEADS * HG_DK
    foxw = FOX_HEADS * FOX_DH
    off_af = 4 * hgw + 3 * foxw
    off_g = off_af + FOX_HEADS

    mod_all = ada_mod(c, ada_w, ada_b).reshape(depth, batch, 6, d)
    x2 = x.reshape(n, d)

    for l in range(depth):
        mod = mod_all[l]
        w7 = w_in[l, :, :off_af].astype(BF16)
        waf = _split_hi_lo(jnp.pad(w_in[l, :, off_af:off_g], ((0, 0), (0, LANES - FOX_HEADS))))
        wg = w_in[l, :, off_g:].astype(BF16)
        hq, hf, hi, hg, aq, ak, av, af, qq, kk, qk = in_proj(x2, mod, w7, waf, seq)

        negf, stats = fox_decay(af, fox_f_bias[l], qq, kk, qk, batch=batch, seq=seq, heads=FOX_HEADS)
        first = fox_first_blocks(stats, heads=FOX_HEADS, sub=min(FOX_T, seq) // min(FOX_TK, seq))
        o_a = hgrn2(hq, hf, hi, hg, hg_lb_logits, hg_norm_w[l], layer=l, batch=batch, seq=seq)
        o_b = fox_attn(first, aq, ak, av, negf, batch=batch, seq=seq)

        wr = _split_hi_lo(jnp.pad(w_router[l], ((0, 0), (0, LANES - N_EXPERTS))))
        br = jnp.pad(b_router[l], (0, LANES - N_EXPERTS), constant_values=NEG_BIG).reshape(1, LANES)
        x1, h2, logits = mixer_out(
            o_a, o_b, x2, mod, wg, w_branch[l].astype(BF16), w_out[l].astype(BF16),
            jnp.stack([ln1_g[l], ln1_b[l]]), wr, br, seq=seq, alpha=alpha)

        ys, rw = moe_grouped(logits, h2, w_gate, b_gate, w_up, b_up, w_down, b_down, layer=l)
        x2 = combine(ys, rw, x1, mod, jnp.stack([ln2_g[l], ln2_b[l]]), seq=seq, alpha=alpha)
    return x2.reshape(batch, seq, d)
```

```python
import functools
import math

import jax
import jax.numpy as jnp
from jax import lax
from jax.experimental import pallas as pl
from jax.experimental.pallas import tpu as pltpu

F32 = jnp.float32
BF16 = jnp.bfloat16
HIGHEST = lax.Precision.HIGHEST

HG_HEADS = 8
HG_DK = 128
FOX_HEADS = 16
FOX_DH = 64
N_EXPERTS = 32
TOP_K = 4
LOG_FLOOR = 1e-30
MASK_VALUE = -1e30
SWIGLU_LIMIT = 7.0
SWIGLU_ALPHA = 1.702
LN_EPS = 1e-5
RMS_EPS = 1e-6
LOG2E = math.log2(math.e)

LANES = 128
BF16_SUBLANES = 16
VMEM_LIMIT = 56 * 1024 * 1024

HG_CHUNK = 64
HG_SUB = 16
HG_TS = 1024
HG_UNROLL = 4
HG_HEADS_PER_STEP = 8
FOX_T = 512
FOX_TK = 512
TM = 512
ROUTE_T = 512
CHUNK = BF16_SUBLANES
BLOCK_CHUNKS = 32
PERM_STRIP = 512
NEG_BIG = -1e30


def _cparams(sem):
    return pltpu.CompilerParams(dimension_semantics=sem, vmem_limit_bytes=VMEM_LIMIT)


def _silu(v):
    return v * jax.nn.sigmoid(v)


def _split3(v):
    a = v.astype(BF16)
    r = v - a.astype(F32)
    b = r.astype(BF16)
    c = (r - b.astype(F32)).astype(BF16)
    return a, b, c


def _dot01(m01, v):
    a, b, c = _split3(v)
    d = lambda t: jnp.dot(m01, t, preferred_element_type=F32)
    return d(a) + d(b) + d(c)


def _dot_split(v, v_hi, w_ref):
    v_lo = (v - v_hi.astype(F32)).astype(BF16)
    d = lambda a, b: jnp.dot(a, b, preferred_element_type=F32)
    return d(v_hi, w_ref[0]) + (d(v_hi, w_ref[1]) + d(v_lo, w_ref[0]))


def _split_hi_lo(w):
    hi = w.astype(BF16)
    return jnp.stack([hi, (w - hi.astype(F32)).astype(BF16)])


def _dot_nt(a, b):
    return lax.dot_general(a, b, (((1,), (1,)), ((), ())), preferred_element_type=F32)


def _dot_tn(a, b):
    return lax.dot_general(a, b, (((0,), (0,)), ((), ())), preferred_element_type=F32)


def _layer_norm(z, g, b):
    mu = jnp.mean(z, axis=-1, keepdims=True)
    zc = z - mu
    var = jnp.mean(zc * zc, axis=-1, keepdims=True)
    return zc * lax.rsqrt(var + LN_EPS) * g + b


def _mod_kernel(c_ref, w_ref, b_ref, o_ref):
    ca = _silu(c_ref[...])
    o_ref[0] = jnp.dot(ca, w_ref[0], precision=HIGHEST, preferred_element_type=F32) + b_ref[0]


def ada_mod(c, ada_w, ada_b):
    depth, d, d6 = ada_w.shape
    b = c.shape[0]
    tn = min(d6, 1536)
    return pl.pallas_call(
        _mod_kernel,
        grid=(depth, d6 // tn),
        in_specs=[pl.BlockSpec((b, d), lambda l, j: (0, 0)),
                  pl.BlockSpec((1, d, tn), lambda l, j: (l, 0, j)),
                  pl.BlockSpec((1, 1, tn), lambda l, j: (l, 0, j))],
        out_specs=pl.BlockSpec((1, b, tn), lambda l, j: (l, 0, j)),
        out_shape=jax.ShapeDtypeStruct((depth, b, d6), F32),
        compiler_params=_cparams(("parallel", "parallel")),
        name="ada_mod",
    )(c, ada_w, ada_b.reshape(depth, 1, d6))


def _hgrn_lower_bound(lg, layer):
    rows = [lg[i:i + 1, :] for i in range(lg.shape[0])]
    mx = functools.reduce(jnp.maximum, rows)
    ex = [jnp.exp(rw - mx) for rw in rows]
    den = functools.reduce(lambda a, b: a + b, ex)
    lb = jnp.zeros_like(mx)
    for i in range(1, layer + 1):
        lb = lb + ex[i] / den
    return jnp.clip(lb, 0.0, 1.0)


def _hgrn_consts():
    rr = lax.broadcasted_iota(jnp.int32, (HG_CHUNK, HG_CHUNK), 0)
    cc = lax.broadcasted_iota(jnp.int32, (HG_CHUNK, HG_CHUNK), 1)
    tril = jnp.where(cc <= rr, 1.0, 0.0).astype(BF16)
    sub8 = lax.broadcasted_iota(jnp.int32, (8, LANES), 0)
    lane8 = lax.broadcasted_iota(jnp.int32, (8, LANES), 1)
    return tril, sub8, lane8


def _hgrn_chunk(qs, sig, iv, gs, lb, nw, st_ref, c_ref, consts):
    C, SUB = HG_CHUNK, HG_SUB
    nsub = C // SUB
    tril, sub8, lane8 = consts
    one_m_lb = 1.0 - lb
    f = lb + one_m_lb * sig
    g = jnp.log2(jnp.maximum(f, LOG_FLOOR))
    k = one_m_lb * (1.0 - sig)
    b = _dot01(tril, g)
    r_end = [b[(j + 1) * SUB - 1:(j + 1) * SUB, :] for j in range(nsub)]
    bc = lambda v: jnp.broadcast_to(v, (SUB, LANES))
    r_prev = jnp.concatenate([jnp.zeros((SUB, LANES), F32)] + [bc(r_end[j]) for j in range(nsub - 1)], axis=0)
    r_own = jnp.concatenate([bc(r_end[j]) for j in range(nsub)], axis=0)
    r_last = r_end[nsub - 1]
    e_prev = jnp.concatenate([jnp.ones((SUB, LANES), F32)]
                             + [bc(jnp.exp2(r_end[j])) for j in range(nsub - 1)], axis=0)
    e_tail = jnp.concatenate([bc(jnp.exp2(r_last - r_end[j])) for j in range(nsub)], axis=0)
    qhat = qs * jnp.exp2(b - r_prev)
    khat = k * jnp.exp2(r_own - b)
    qtil = qhat * e_prev
    kst = khat * e_tail

    lhs, rhs = [], []
    for j in range(nsub - 1):
        lo = (j + 1) * SUB
        qj = qs[lo:, :] * jnp.exp2(b[lo:, :] - r_end[j])
        lhs.append(jnp.concatenate([jnp.zeros((lo, LANES), F32), qj], axis=0))
        parts = []
        if j > 0:
            parts.append(jnp.zeros((j * SUB, LANES), F32))
        parts.append(khat[j * SUB:(j + 1) * SUB, :])
        parts.append(jnp.zeros((C - (j + 1) * SUB, LANES), F32))
        rhs.append(jnp.concatenate(parts, axis=0))
    a_off = _dot_nt(jnp.concatenate(lhs, axis=1).astype(BF16),
                    jnp.concatenate(rhs, axis=1).astype(BF16))

    c_ref[...] = b - jnp.log2(k)
    pieces = []
    for blk in range(nsub):
        base = blk * SUB
        bt = [b[base:base + 8, :], b[base + 8:base + 16, :]]
        qt = [qs[base:base + 8, :], qs[base + 8:base + 16, :]]
        acc = [jnp.zeros((8, LANES), F32), jnp.zeros((8, LANES), F32)]
        for s in range(SUB):
            cs = jnp.broadcast_to(c_ref[base + s:base + s + 1, :], (8, LANES))
            for half in range(2):
                if s >= 8 and half == 0:
                    continue
                d = bt[half] - cs
                s_loc = s - 8 * half
                if s_loc >= 0:
                    d = jnp.where(sub8 >= s_loc, d, NEG_BIG)
                xval = jnp.exp2(d) * qt[half]
                col = jnp.sum(xval, axis=-1, keepdims=True)
                acc[half] = jnp.where(lane8 == base + s, col, acc[half])
        pieces += acc
    a_diag = jnp.concatenate(pieces, axis=0)
    a = (a_off + a_diag[:, :C]).astype(BF16)

    st = st_ref[...]
    o = jnp.dot(a, iv, preferred_element_type=F32) + _dot_nt(qtil.astype(BF16), st.astype(BF16))
    st_ref[...] = st * jnp.exp2(r_last) + _dot_tn(iv, kst.astype(BF16))

    ms = jnp.mean(o * o, axis=-1, keepdims=True)
    return o * lax.rsqrt(ms + RMS_EPS) * nw * gs


def _in_proj_kernel(x_ref, mod_ref, w_ref, waf_ref, sel_ref, hq_ref, hf_ref, hi_ref, hg_ref,
                    aq_ref, ak_ref, av_ref, af_ref, qq_ref, kk_ref, qk_ref, *, d):
    shift = mod_ref[0, 0:1, :]
    scale = mod_ref[0, 1:2, :]
    h = x_ref[...] * (1.0 + scale) + shift
    hb = h.astype(BF16)

    def proj(g):
        return jnp.dot(hb, w_ref[:, g * d:(g + 1) * d], preferred_element_type=F32)

    hq_ref[...] = _silu(proj(0)).astype(BF16)
    hf_ref[...] = jax.nn.sigmoid(proj(1))
    hi_ref[...] = proj(2).astype(BF16)
    hg_ref[...] = _silu(proj(3)).astype(BF16)
    qb = (proj(4) * (FOX_DH ** -0.5 * LOG2E)).astype(BF16)
    kb = proj(5).astype(BF16)
    aq_ref[...] = qb
    ak_ref[...] = kb
    av_ref[...] = proj(6).astype(BF16)
    af_ref[...] = _dot_split(h, hb, waf_ref)
    per_head = lambda v: jnp.dot(v, sel_ref[...], preferred_element_type=F32)
    qq_ref[...] = per_head(qb * qb)
    kk_ref[...] = per_head(kb * kb)
    qk_ref[...] = per_head(qb * kb)


def in_proj(x2, mod, w7, waf, seq):
    n, d = x2.shape
    tpb = seq // TM
    row = lambda i: (i, 0)
    big = lambda dt: jax.ShapeDtypeStruct((n, d), dt)
    small = jax.ShapeDtypeStruct((n, LANES), F32)
    sel = (jnp.arange(d)[:, None] // FOX_DH == jnp.arange(LANES)[None, :]).astype(BF16)
    return pl.pallas_call(
        functools.partial(_in_proj_kernel, d=d),
        grid=(n // TM,),
        in_specs=[pl.BlockSpec((TM, d), row),
                  pl.BlockSpec((1, 6, d), lambda i: (i // tpb, 0, 0)),
                  pl.BlockSpec((d, 7 * d), lambda i: (0, 0), pipeline_mode=pl.Buffered(1)),
                  pl.BlockSpec((2, d, LANES), lambda i: (0, 0, 0), pipeline_mode=pl.Buffered(1)),
                  pl.BlockSpec((d, LANES), lambda i: (0, 0), pipeline_mode=pl.Buffered(1))],
        out_specs=[pl.BlockSpec((TM, d), row)] * 7 + [pl.BlockSpec((TM, LANES), row)] * 4,
        out_shape=[big(BF16), big(F32), big(BF16), big(BF16), big(BF16), big(BF16), big(BF16),
                   small, small, small, small],
        compiler_params=_cparams(("parallel",)),
        name="in_proj",
    )(x2, mod, w7, waf, sel)


def _hgrn_kernel(q_ref, f_ref, i_ref, g_ref, lbl_ref, nw_ref, o_ref, st_ref, c_ref, *, layer, ts, hps):
    C = HG_CHUNK

    @pl.when(pl.program_id(2) == 0)
    def _():
        st_ref[...] = jnp.zeros_like(st_ref)

    lb_all = _hgrn_lower_bound(lbl_ref[...], layer)
    nw = nw_ref[...]
    consts = _hgrn_consts()

    def chunk(ci, carry):
        for u in range(HG_UNROLL):
            r0 = pl.multiple_of((ci * HG_UNROLL + u) * C, C)
            for hh in range(hps):
                sl = slice(hh * HG_DK, (hh + 1) * HG_DK)
                o = _hgrn_chunk(q_ref[pl.ds(r0, C), sl].astype(F32), f_ref[pl.ds(r0, C), sl],
                                i_ref[pl.ds(r0, C), sl], g_ref[pl.ds(r0, C), sl].astype(F32),
                                lb_all[:, sl], nw, st_ref.at[hh], c_ref.at[u * hps + hh], consts)
                o_ref[pl.ds(r0, C), sl] = o.astype(BF16)
        return carry

    lax.fori_loop(0, ts // (C * HG_UNROLL), chunk, 0)


def hgrn2(hq, hf, hi, hg, lb_logits, norm_w, *, layer, batch, seq):
    n, width = hq.shape
    hps = HG_HEADS_PER_STEP
    groups = width // (hps * HG_DK)
    ts = min(HG_TS, seq)
    spb = seq // ts
    depth = lb_logits.shape[0]
    blk = pl.BlockSpec((ts, hps * HG_DK), lambda b, h, t: (b * spb + t, h))
    return pl.pallas_call(
        functools.partial(_hgrn_kernel, layer=layer, ts=ts, hps=hps),
        grid=(batch, groups, spb),
        in_specs=[blk, blk, blk, blk,
                  pl.BlockSpec((depth, hps * HG_DK), lambda b, h, t: (0, h)),
                  pl.BlockSpec((1, HG_DK), lambda b, h, t: (0, 0))],
        out_specs=blk,
        out_shape=jax.ShapeDtypeStruct((n, width), BF16),
        scratch_shapes=[pltpu.VMEM((hps, HG_DK, HG_DK), F32), pltpu.VMEM((HG_UNROLL * hps, HG_CHUNK, HG_DK), F32)],
        compiler_params=_cparams(("parallel", "parallel", "arbitrary")),
        name="hgrn2",
    )(hq, hf, hi, hg, lb_logits, norm_w.reshape(1, HG_DK))


def _fox_decay_kernel(af_ref, bias_ref, qq_ref, kk_ref, qk_ref, nf_ref, st_ref, carry_ref, *, t, tk, heads):
    @pl.when(pl.program_id(1) == 0)
    def _():
        carry_ref[...] = jnp.zeros_like(carry_ref)

    z = af_ref[...] + bias_ref[...]
    ls = jnp.minimum(z, 0.0) - jnp.log(1.0 + jnp.exp(-jnp.abs(z)))
    rr = lax.broadcasted_iota(jnp.int32, (t, t), 0)
    cc = lax.broadcasted_iota(jnp.int32, (t, t), 1)
    tril = jnp.where(cc <= rr, 1.0, 0.0).astype(BF16)
    cs = _dot01(tril, ls) + carry_ref[...]
    carry_ref[...] = cs[t - 1:t, :]
    nf = cs * (-LOG2E)
    nf_ref[0, :, 0, :] = nf.T[:heads, :]

    amax = jnp.max(jnp.sqrt(qq_ref[...]), axis=0, keepdims=True)
    dmin = jnp.min(qk_ref[...] + nf, axis=0, keepdims=True)
    kn = jnp.sqrt(kk_ref[...])
    sub = t // tk
    kmax = [jnp.max(kn[s * tk:(s + 1) * tk, :], axis=0, keepdims=True) for s in range(sub)]
    ends = [nf[(s + 1) * tk - 1:(s + 1) * tk, :] for s in range(sub)]
    pad = jnp.zeros((8 - 2 - 2 * sub, LANES), F32)
    st_ref[0, 0] = jnp.concatenate([amax, dmin] + kmax + ends + [pad], axis=0)


def fox_decay(af, bias, qq, kk, qk, *, batch, seq, heads):
    t = min(FOX_T, seq)
    nblk = seq // t
    row = pl.BlockSpec((t, LANES), lambda b, i: (b * nblk + i, 0))
    return pl.pallas_call(
        functools.partial(_fox_decay_kernel, t=t, tk=min(FOX_TK, t), heads=heads),
        grid=(batch, nblk),
        in_specs=[row, pl.BlockSpec((1, LANES), lambda b, i: (0, 0)), row, row, row],
        out_specs=[pl.BlockSpec((1, heads, 1, t), lambda b, i: (b, 0, 0, i)),
                   pl.BlockSpec((1, 1, 8, LANES), lambda b, i: (b, i, 0, 0))],
        out_shape=[jax.ShapeDtypeStruct((batch, heads, 1, seq), F32),
                   jax.ShapeDtypeStruct((batch, nblk, 8, LANES), F32)],
        scratch_shapes=[pltpu.VMEM((1, LANES), F32)],
        compiler_params=_cparams(("parallel", "arbitrary")),
        name="fox_decay",
    )(af, jnp.pad(bias, (0, LANES - heads)).reshape(1, LANES), qq, kk, qk)


FOX_SKIP_LOG2 = 170.0


def fox_first_blocks(stats, *, heads, sub):
    nb, nblk = stats.shape[0], stats.shape[1]
    amax, dmin = stats[:, :, 0, :heads], stats[:, :, 1, :heads]
    kmax = stats[:, :, 2:2 + sub, :heads].reshape(nb, nblk * sub, heads)
    e_end = stats[:, :, 2 + sub:2 + 2 * sub, :heads].reshape(nb, nblk * sub, heads)
    ub = amax[:, :, None, :] * kmax[:, None, :, :] * 1.02 + 2.0 - dmin[:, :, None, :] + e_end[:, None, :, :]
    earlier = (jnp.arange(nblk) * sub)[None, :, None, None] > jnp.arange(nblk * sub)[None, None, :, None]
    skip = (ub < -FOX_SKIP_LOG2) & earlier
    first = jnp.sum(jnp.cumprod(skip.astype(jnp.int32), axis=2), axis=2)
    first = jnp.min(first.reshape(first.shape[0], nblk, heads // 2, 2), axis=-1)
    return first.transpose(0, 2, 1).reshape(-1).astype(jnp.int32)


def _fox_kernel(first_ref, q_ref, k_ref, v_ref, nf_ref, o_ref, vaug_ref, acc_ref, m_ref, za_ref, zb_ref,
                *, t, tk, seq):
    nq = seq // t
    sub = t // tk
    pair_id = pl.program_id(0) * pl.num_programs(1) + pl.program_id(1)

    vaug_ref[:, :LANES] = v_ref[...]
    vaug_ref[:, LANES:] = jnp.ones((seq, LANES), BF16)

    lane = lax.broadcasted_iota(jnp.int32, (t, LANES), 1)
    row = lax.broadcasted_iota(jnp.int32, (t, tk), 0)
    col = lax.broadcasted_iota(jnp.int32, (t, tk), 1)

    def q_block(qi, carry):
        r0 = pl.multiple_of(qi * t, t)
        q = q_ref[pl.ds(r0, t), :]
        zero = jnp.zeros_like(q)
        qh = [jnp.where(lane < FOX_DH, q, zero), jnp.where(lane >= FOX_DH, q, zero)]
        for h in range(2):
            acc_ref[h] = jnp.zeros((t, 2 * LANES), F32)
            m_ref[h] = jnp.full((t, LANES), -jnp.inf, F32)

        def logits(j, z_ref):
            c0 = pl.multiple_of(j * tk, tk)
            kb = k_ref[pl.ds(c0, tk), :]
            for h in range(2):
                z_ref[h] = _dot_nt(qh[h], kb) + nf_ref[0, h, :, pl.ds(c0, tk)]

        def consume(j, z_ref, diag):
            c0 = pl.multiple_of(j * tk, tk)
            vb = vaug_ref[pl.ds(c0, tk), :]
            for h in range(2):
                z = z_ref[h]
                if diag is not None:
                    z = jnp.where(col + diag * tk <= row, z, MASK_VALUE)
                m_old = m_ref[h]
                m_new = jnp.maximum(m_old, jnp.max(z, axis=-1, keepdims=True))
                alpha = jnp.exp2(m_old - m_new)
                p = jnp.exp2((z - jnp.concatenate([m_new] * (tk // LANES), axis=1)).astype(BF16))
                pv = jnp.dot(p, vb, preferred_element_type=F32)
                acc_ref[h] = jnp.concatenate([alpha, alpha], axis=1) * acc_ref[h] + pv
                m_ref[h] = m_new

        d0 = qi * sub
        start = jnp.minimum(first_ref[pair_id * nq + qi], d0)
        before = d0 - start
        logits(start, za_ref)

        def pair(p, c):
            j = start + 2 * p
            logits(j + 1, zb_ref)
            consume(j, za_ref, None)
            logits(j + 2, za_ref)
            consume(j + 1, zb_ref, None)
            return c

        lax.fori_loop(0, before // 2, pair, 0)

        def diagonal(cur, nxt):
            for s in range(sub):
                if s + 1 < sub:
                    logits(d0 + s + 1, nxt)
                consume(d0 + s, cur, s)
                cur, nxt = nxt, cur

        @pl.when(before % 2 == 1)
        def _():
            logits(d0, zb_ref)
            consume(d0 - 1, za_ref, None)
            diagonal(zb_ref, za_ref)

        @pl.when(before % 2 == 0)
        def _():
            diagonal(za_ref, zb_ref)

        a0 = acc_ref[0]
        a1 = acc_ref[1]
        o0 = a0[:, :LANES] / a0[:, LANES:]
        o1 = a1[:, :LANES] / a1[:, LANES:]
        o_ref[pl.ds(r0, t), :] = jnp.where(lane < FOX_DH, o0, o1).astype(BF16)
        return carry

    lax.fori_loop(0, nq, q_block, 0)


def fox_attn(first, aq, ak, av, negf, *, batch, seq):
    n, width = aq.shape
    pairs = width // LANES
    t = min(FOX_T, seq)
    tk = min(FOX_TK, t)
    whole = pl.BlockSpec((seq, LANES), lambda b, p, f: (b, p))
    grid_spec = pltpu.PrefetchScalarGridSpec(
        num_scalar_prefetch=1,
        grid=(batch, pairs),
        in_specs=[whole, whole, whole, pl.BlockSpec((1, 2, 1, seq), lambda b, p, f: (b, p, 0, 0))],
        out_specs=whole,
        scratch_shapes=[pltpu.VMEM((seq, 2 * LANES), BF16), pltpu.VMEM((2, t, 2 * LANES), F32),
                        pltpu.VMEM((2, t, LANES), F32), pltpu.VMEM((2, t, tk), F32), pltpu.VMEM((2, t, tk), F32)],
    )
    return pl.pallas_call(
        functools.partial(_fox_kernel, t=t, tk=tk, seq=seq),
        grid_spec=grid_spec,
        out_shape=jax.ShapeDtypeStruct((n, width), BF16),
        compiler_params=_cparams(("parallel", "parallel")),
        name="fox_attn",
    )(first, aq, ak, av, negf)


def _mixer_out_kernel(oa_ref, ob_ref, x_ref, mod_ref, wg_ref, wbr_ref, wo_ref, ln_ref, wr_ref, br_ref,
                      x1_ref, h2_ref, lg_ref, *, d, alpha):
    shift1, scale1, gate1 = mod_ref[0, 0:1, :], mod_ref[0, 1:2, :], mod_ref[0, 2:3, :]
    shift2, scale2 = mod_ref[0, 3:4, :], mod_ref[0, 4:5, :]
    x = x_ref[...]
    hb = (x * (1.0 + scale1) + shift1).astype(BF16)
    ga = jnp.dot(hb, wg_ref[:, :d], preferred_element_type=F32)
    gb = jnp.dot(hb, wg_ref[:, d:], preferred_element_type=F32)
    pa = jnp.dot(oa_ref[...], wbr_ref[:d, :], preferred_element_type=F32)
    pb = jnp.dot(ob_ref[...], wbr_ref[d:, :], preferred_element_type=F32)
    merged = jax.nn.sigmoid(ga) * pa + jax.nn.sigmoid(gb) * pb
    y = jnp.dot(merged.astype(BF16), wo_ref[...], preferred_element_type=F32)
    x1 = _layer_norm(alpha * x + (1.0 + gate1) * y, ln_ref[0:1, :], ln_ref[1:2, :])
    x1_ref[...] = x1
    h2 = x1 * (1.0 + scale2) + shift2
    h2b = h2.astype(BF16)
    h2_ref[...] = h2b
    lg_ref[...] = _dot_split(h2, h2b, wr_ref) + br_ref[...]


def mixer_out(oa, ob, x2, mod, wg, wbr, wo, ln, wr, br, *, seq, alpha):
    n, d = x2.shape
    tpb = seq // TM
    row = lambda i: (i, 0)
    const = lambda shape: pl.BlockSpec(shape, lambda i: (0, 0), pipeline_mode=pl.Buffered(1))
    return pl.pallas_call(
        functools.partial(_mixer_out_kernel, d=d, alpha=alpha),
        grid=(n // TM,),
        in_specs=[pl.BlockSpec((TM, d), row), pl.BlockSpec((TM, d), row), pl.BlockSpec((TM, d), row),
                  pl.BlockSpec((1, 6, d), lambda i: (i // tpb, 0, 0)),
                  const((d, 2 * d)), const((2 * d, d)), const((d, d)), const((2, d)),
                  pl.BlockSpec((2, d, LANES), lambda i: (0, 0, 0), pipeline_mode=pl.Buffered(1)),
                  const((1, LANES))],
        out_specs=[pl.BlockSpec((TM, d), row), pl.BlockSpec((TM, d), row), pl.BlockSpec((TM, LANES), row)],
        out_shape=[jax.ShapeDtypeStruct((n, d), F32), jax.ShapeDtypeStruct((n, d), BF16),
                   jax.ShapeDtypeStruct((n, LANES), F32)],
        compiler_params=_cparams(("parallel",)),
        name="mixer_out",
    )(oa, ob, x2, mod, wg, wbr, wo, ln, wr, br)


def _tile_rows(t):
    rows = t * TOP_K + N_EXPERTS * (CHUNK - 1)
    return -(-rows // (8 * CHUNK)) * (8 * CHUNK)


def _dispatch_kernel(lg_ref, h2_ref, xs_ref, ys0_ref, rw_ref, cnt_ref, *, t, rows, nt):
    ys0_ref[...] = jnp.zeros_like(ys0_ref)

    @pl.when(pl.program_id(0) < nt)
    def _():
        _route_tile(lg_ref, h2_ref, xs_ref, rw_ref, cnt_ref, t=t, rows=rows)


def _route_tile(lg_ref, h2_ref, xs_ref, rw_ref, cnt_ref, *, t, rows):
    lane = lax.broadcasted_iota(jnp.int32, (t, LANES), 1).astype(F32)
    lg = lg_ref[...]
    onehots, vals = [], []
    for _ in range(TOP_K):
        mx = jnp.max(lg, axis=-1, keepdims=True)
        idx = jnp.min(jnp.where(lg == mx, lane, float(LANES)), axis=-1, keepdims=True)
        oh = lane == idx
        onehots.append(oh)
        vals.append(mx)
        lg = jnp.where(oh, -jnp.inf, lg)
    es = [jnp.exp(v - vals[0]) for v in vals]
    den = functools.reduce(lambda a, b: a + b, es)
    ws = [e / den for e in es]

    member = functools.reduce(lambda a, b: a + b, [jnp.where(oh, 1.0, 0.0) for oh in onehots])
    rr = lax.broadcasted_iota(jnp.int32, (t, t), 0)
    cc = lax.broadcasted_iota(jnp.int32, (t, t), 1)
    strict_lower = jnp.where(cc < rr, 1.0, 0.0).astype(BF16)
    rank = jnp.dot(strict_lower, member.astype(BF16), preferred_element_type=F32)
    cnt = jnp.sum(member, axis=0, keepdims=True)
    nch = jnp.floor((cnt + (CHUNK - 1)) * (1.0 / CHUNK))
    ur = lax.broadcasted_iota(jnp.int32, (LANES, LANES), 0)
    uc = lax.broadcasted_iota(jnp.int32, (LANES, LANES), 1)
    strict_upper = jnp.where(ur < uc, 1.0, 0.0).astype(BF16)
    toff = jnp.dot(jnp.broadcast_to(nch, (8, LANES)).astype(BF16), strict_upper,
                   preferred_element_type=F32)[0:1, :]
    pos = toff * float(CHUNK) + rank

    rw = jnp.full((t, LANES), -1.0, F32)
    for kk in range(TOP_K):
        r_k = jnp.sum(jnp.where(onehots[kk], pos, 0.0), axis=-1, keepdims=True)
        rw = jnp.where(lane == kk, r_k, rw)
        rw = jnp.where(lane == TOP_K + kk, ws[kk], rw)
    rw_ref[...] = rw
    cnt_ref[0] = jnp.broadcast_to(cnt, (8, LANES))

    rt = rw.T
    h2 = h2_ref[...]
    for r0 in range(0, rows, PERM_STRIP):
        riota = (lax.broadcasted_iota(jnp.int32, (PERM_STRIP, t), 0) + r0).astype(F32)
        pm = jnp.zeros((PERM_STRIP, t), F32)
        for kk in range(TOP_K):
            pm = jnp.where(riota == rt[kk:kk + 1, :], 1.0, pm)
        xs_ref[r0:r0 + PERM_STRIP, :] = jnp.dot(pm.astype(BF16), h2, preferred_element_type=F32).astype(BF16)


def dispatch(logits, h2):
    n, d = h2.shape
    t = min(ROUTE_T, n)
    rows = _tile_rows(t)
    nt = n // t
    assert 2 * BLOCK_CHUNKS * CHUNK <= rows
    tile = lambda i: (jnp.minimum(i, nt - 1), 0)
    return pl.pallas_call(
        functools.partial(_dispatch_kernel, t=t, rows=rows, nt=nt),
        grid=(nt + 1,),
        in_specs=[pl.BlockSpec((t, LANES), tile), pl.BlockSpec((t, d), tile)],
        out_specs=[pl.BlockSpec((rows, d), tile), pl.BlockSpec((rows, d), lambda i: (i, 0)),
                   pl.BlockSpec((t, LANES), tile),
                   pl.BlockSpec((1, 8, LANES), lambda i: (jnp.minimum(i, nt - 1), 0, 0))],
        out_shape=[jax.ShapeDtypeStruct((nt * rows, d), BF16),
                   jax.ShapeDtypeStruct(((nt + 1) * rows, d), BF16),
                   jax.ShapeDtypeStruct((n, LANES), F32), jax.ShapeDtypeStruct((nt, 8, LANES), F32)],
        compiler_params=_cparams(("arbitrary",)),
        name="dispatch",
    )(logits, h2)


def _num_blocks(nt, t):
    max_chunks = nt * ((t * TOP_K + N_EXPERTS * (CHUNK - 1)) // CHUNK)
    return -(-max_chunks // BLOCK_CHUNKS) + N_EXPERTS


def _routing_tables(cnt, t):
    nt = cnt.shape[0]
    e = N_EXPERTS
    cpt = _tile_rows(t) // CHUNK
    nb = _num_blocks(nt, t)
    nch = (cnt + (CHUNK - 1)) // CHUNK
    toff = jnp.cumsum(nch, axis=1) - nch
    n_e = jnp.sum(nch, axis=0)
    nblk = (n_e + (BLOCK_CHUNKS - 1)) // BLOCK_CHUNKS
    bend = jnp.cumsum(nblk)
    bstart = bend - nblk
    total_blocks = bend[-1]
    cumj_incl = jnp.cumsum(nch, axis=0)
    cumj = cumj_incl - nch

    b = jnp.arange(nb, dtype=jnp.int32)
    last = jnp.maximum(total_blocks - 1, 0)
    bex = jnp.minimum(jnp.sum(bend[None, :] <= jnp.minimum(b, last)[:, None], axis=1), e - 1).astype(jnp.int32)
    bvalid = b < total_blocks
    sel = (bex[:, None] == jnp.arange(e, dtype=jnp.int32)[None, :]).astype(jnp.int32)
    row = lambda tab: jnp.sum(sel[:, :, None] * tab.T[None, :, :], axis=1)
    cum_b, excl_b, toff_b = row(cumj_incl), row(cumj), row(toff)
    n_b = jnp.sum(sel * n_e[None, :], axis=1)
    s0 = (b - jnp.sum(sel * bstart[None, :], axis=1)) * BLOCK_CHUNKS
    s = s0[:, None] + jnp.arange(BLOCK_CHUNKS, dtype=jnp.int32)[None, :]
    valid = bvalid[:, None] & (s < n_b[:, None])
    j = jnp.minimum(jnp.sum(cum_b[:, None, :] <= s[:, :, None], axis=2), nt - 1)
    jsel = (j[:, :, None] == jnp.arange(nt, dtype=jnp.int32)[None, None, :]).astype(jnp.int32)
    pick = lambda tab_b: jnp.sum(jsel * tab_b[:, None, :], axis=2)
    chunk = j * cpt + pick(toff_b) + (s - pick(excl_b))
    src = jnp.where(valid, chunk, 0).astype(jnp.int32).reshape(-1)
    spare = nt * cpt + (b[:, None] % 2) * BLOCK_CHUNKS + jnp.arange(BLOCK_CHUNKS, dtype=jnp.int32)[None, :]
    dst = jnp.where(valid, chunk, spare).astype(jnp.int32).reshape(-1)
    return bex, bvalid.astype(jnp.int32), src, dst, nb


def _expert_kernel(bex_ref, bvalid_ref, src_ref, dst_ref, xs_ref, wg_ref, bg_ref, wu_ref, bu_ref, wd_ref, bd_ref,
                   ys_in_ref, ys_ref, xbuf, ybuf, wbf, sem_in, sem, *, nb):
    del ys_in_ref
    i = pl.program_id(0)
    slot = i % 2

    def in_copy(step, sl, c):
        row = pl.multiple_of(src_ref[step * BLOCK_CHUNKS + c] * CHUNK, CHUNK)
        return pltpu.make_async_copy(xs_ref.at[pl.ds(row, CHUNK), :],
                                     xbuf.at[sl, pl.ds(c * CHUNK, CHUNK), :], sem_in.at[sl])

    def out_copy(step, sl, c):
        row = pl.multiple_of(dst_ref[step * BLOCK_CHUNKS + c] * CHUNK, CHUNK)
        return pltpu.make_async_copy(ybuf.at[sl, pl.ds(c * CHUNK, CHUNK), :],
                                     ys_ref.at[pl.ds(row, CHUNK), :], sem.at[sl])

    def gather(step, sl):
        @pl.when(bvalid_ref[step] == 1)
        def _():
            for c in range(BLOCK_CHUNKS):
                in_copy(step, sl, c).start()

    @pl.when(i == 0)
    def _():
        gather(0, 0)

    @pl.when(i + 1 < nb)
    def _():
        gather(jnp.minimum(i + 1, nb - 1), 1 - slot)

    def wait_step(step, sl):
        @pl.when(bvalid_ref[step] == 1)
        def _():
            for c in range(BLOCK_CHUNKS):
                out_copy(step, sl, c).wait()

    @pl.when(i >= 2)
    def _():
        wait_step(jnp.maximum(i - 2, 0), slot)

    prev = jnp.maximum(i - 1, 0)
    new_expert = (i == 0) | (bex_ref[i] != bex_ref[prev])

    @pl.when((bvalid_ref[i] == 1) & new_expert)
    def _():
        wbf[0] = wg_ref[0, 0].astype(BF16)
        wbf[1] = wu_ref[0, 0].astype(BF16)
        wbf[2] = wd_ref[0, 0].astype(BF16)

    @pl.when(bvalid_ref[i] == 1)
    def _():
        for c in range(BLOCK_CHUNKS):
            in_copy(i, slot, c).wait()
        xb = xbuf[slot]
        g = jnp.minimum(jnp.dot(xb, wbf[0], preferred_element_type=F32) + bg_ref[0, 0], SWIGLU_LIMIT)
        u = jnp.clip(jnp.dot(xb, wbf[1], preferred_element_type=F32) + bu_ref[0, 0], -SWIGLU_LIMIT, SWIGLU_LIMIT)
        a = (u + 1.0) * (g * jax.nn.sigmoid(SWIGLU_ALPHA * g))
        y = jnp.dot(a.astype(BF16), wbf[2], preferred_element_type=F32) + bd_ref[0, 0]
        ybuf[slot] = y.astype(BF16)
        for c in range(BLOCK_CHUNKS):
            out_copy(i, slot, c).start()

    @pl.when(i == nb - 1)
    def _():
        @pl.when(i >= 1)
        def _():
            wait_step(jnp.maximum(i - 1, 0), 1 - slot)
        wait_step(i, slot)


def experts(xs, ys_init, bex, bvalid, src, dst, w_gate, b_gate, w_up, b_up, w_down, b_down, *, nb, layer):
    d = xs.shape[1]
    depth, e, _, de = w_gate.shape
    rows = BLOCK_CHUNKS * CHUNK

    wspec = lambda shape: pl.BlockSpec((1,) + shape, lambda i, be, bv, sc, dc: (layer, be[i], 0, 0))
    hbm = pl.BlockSpec(memory_space=pl.ANY)
    grid_spec = pltpu.PrefetchScalarGridSpec(
        num_scalar_prefetch=4,
        grid=(nb,),
        in_specs=[hbm, wspec((1, d, de)), wspec((1, 1, de)), wspec((1, d, de)), wspec((1, 1, de)),
                  wspec((1, de, d)), wspec((1, 1, d)), hbm],
        out_specs=hbm,
        scratch_shapes=[pltpu.VMEM((2, rows, d), BF16), pltpu.VMEM((2, rows, d), BF16),
                        pltpu.VMEM((3, d, de), BF16), pltpu.SemaphoreType.DMA((2,)),
                        pltpu.SemaphoreType.DMA((2,))],
    )
    n_in = 4 + 1 + 6
    return pl.pallas_call(
        functools.partial(_expert_kernel, nb=nb),
        grid_spec=grid_spec,
        out_shape=jax.ShapeDtypeStruct(ys_init.shape, BF16),
        input_output_aliases={n_in: 0},
        compiler_params=_cparams(("arbitrary",)),
        name="experts",
    )(bex, bvalid, src, dst, xs,
      w_gate, b_gate.reshape(depth, e, 1, de), w_up, b_up.reshape(depth, e, 1, de),
      w_down, b_down.reshape(depth, e, 1, d), ys_init)


def _combine_kernel(ys_ref, rw_ref, x1_ref, mod_ref, ln_ref, o_ref, *, t, rows, alpha):
    gate2 = mod_ref[0, 5:6, :]
    rw = rw_ref[...]
    y2 = jnp.zeros((t, x1_ref.shape[1]), F32)
    for c0 in range(0, rows, PERM_STRIP):
        liota = (lax.broadcasted_iota(jnp.int32, (t, PERM_STRIP), 1) + c0).astype(F32)
        wm = jnp.zeros((t, PERM_STRIP), F32)
        for kk in range(TOP_K):
            wm = jnp.where(liota == rw[:, kk:kk + 1], rw[:, TOP_K + kk:TOP_K + kk + 1], wm)
        y2 = y2 + jnp.dot(wm.astype(BF16), ys_ref[c0:c0 + PERM_STRIP, :], preferred_element_type=F32)
    o_ref[...] = _layer_norm(alpha * x1_ref[...] + (1.0 + gate2) * y2, ln_ref[0:1, :], ln_ref[1:2, :])


def combine(ys, rw, x1, mod, ln, *, seq, alpha):
    n, d = x1.shape
    t = min(ROUTE_T, n)
    rows = _tile_rows(t)
    tpb = max(seq // t, 1)
    return pl.pallas_call(
        functools.partial(_combine_kernel, t=t, rows=rows, alpha=alpha),
        grid=(n // t,),
        in_specs=[pl.BlockSpec((rows, d), lambda i: (i, 0)),
                  pl.BlockSpec((t, LANES), lambda i: (i, 0)),
                  pl.BlockSpec((t, d), lambda i: (i, 0)),
                  pl.BlockSpec((1, 6, d), lambda i: (i // tpb, 0, 0)),
                  pl.BlockSpec((2, d), lambda i: (0, 0))],
        out_specs=pl.BlockSpec((t, d), lambda i: (i, 0)),
        out_shape=jax.ShapeDtypeStruct((n, d), F32),
        compiler_params=_cparams(("parallel",)),
        name="combine",
    )(ys, rw, x1, mod, ln)


def moe_grouped(logits, h2, w_gate, b_gate, w_up, b_up, w_down, b_down, *, layer):
    n, d = h2.shape
    t = min(ROUTE_T, n)
    xs, ys_init, rw, cnt = dispatch(logits, h2)
    cnt_i = cnt[:, 0, :N_EXPERTS].astype(jnp.int32)
    bex, bvalid, src, dst, nb = _routing_tables(cnt_i, t)
    ys = experts(xs, ys_init, bex, bvalid, src, dst, w_gate, b_gate, w_up, b_up, w_down, b_down,
                 nb=nb, layer=layer)
    return ys, rw


def kernel(x, c, w_in, fox_f_bias, hg_lb_logits, hg_norm_w, w_branch, w_out, ada_w, ada_b,
           ln1_g, ln1_b, w_router, b_router, w_gate, b_gate, w_up, b_up, w_down, b_down,
           ln2_g, ln2_b):
    batch, seq, d = x.shape
    depth = w_in.shape[0]
    n = batch * seq
    alpha = (2 * depth) ** 0.25
    hgw = HG_HEADS * HG_DK
    foxw = FOX_HEADS * FOX_DH
    off_af = 4 * hgw + 3 * foxw
    off_g = off_af + FOX_HEADS

    mod_all = ada_mod(c, ada_w, ada_b).reshape(depth, batch, 6, d)
    x2 = x.reshape(n, d)

    for l in range(depth):
        mod = mod_all[l]
        w7 = w_in[l, :, :off_af].astype(BF16)
        waf = _split_hi_lo(jnp.pad(w_in[l, :, off_af:off_g], ((0, 0), (0, LANES - FOX_HEADS))))
        wg = w_in[l, :, off_g:].astype(BF16)
        hq, hf, hi, hg, aq, ak, av, af, qq, kk, qk = in_proj(x2, mod, w7, waf, seq)

        negf, stats = fox_decay(af, fox_f_bias[l], qq, kk, qk, batch=batch, seq=seq, heads=FOX_HEADS)
        first = fox_first_blocks(stats, heads=FOX_HEADS, sub=min(FOX_T, seq) // min(FOX_TK, seq))
        o_a = hgrn2(hq, hf, hi, hg, hg_lb_logits, hg_norm_w[l], layer=l, batch=batch, seq=seq)
        o_b = fox_attn(first, aq, ak, av, negf, batch=batch, seq=seq)

        wr = _split_hi_lo(jnp.pad(w_router[l], ((0, 0), (0, LANES - N_EXPERTS))))
        br = jnp.pad(b_router[l], (0, LANES - N_EXPERTS), constant_values=NEG_BIG).reshape(1, LANES)
        x1, h2, logits = mixer_out(
            o_a, o_b, x2, mod, wg, w_branch[l].astype(BF16), w_out[l].astype(BF16),
            jnp.stack([ln1_g[l], ln1_b[l]]), wr, br, seq=seq, alpha=alpha)

        ys, rw = moe_grouped(logits, h2, w_gate, b_gate, w_up, b_up, w_down, b_down, layer=l)
        x2 = combine(ys, rw, x1, mod, jnp.stack([ln2_g[l], ln2_b[l]]), seq=seq, alpha=alpha)
    return x2.reshape(batch, seq, d)
```

```python
import functools
import math

import jax
import jax.numpy as jnp
from jax import lax
from jax.experimental import pallas as pl
from jax.experimental.pallas import tpu as pltpu

F32 = jnp.float32
BF16 = jnp.bfloat16
HIGHEST = lax.Precision.HIGHEST

HG_HEADS = 8
HG_DK = 128
FOX_HEADS = 16
FOX_DH = 64
N_EXPERTS = 32
TOP_K = 4
LOG_FLOOR = 1e-30
MASK_VALUE = -1e30
SWIGLU_LIMIT = 7.0
SWIGLU_ALPHA = 1.702
LN_EPS = 1e-5
RMS_EPS = 1e-6
LOG2E = math.log2(math.e)

LANES = 128
BF16_SUBLANES = 16
VMEM_LIMIT = 56 * 1024 * 1024

HG_CHUNK = 64
HG_SUB = 16
HG_TS = 1024
HG_UNROLL = 4
HG_HEADS_PER_STEP = 8
FOX_T = 512
FOX_TK = 512
TM = 512
ROUTE_T = 512
CHUNK = BF16_SUBLANES
BLOCK_CHUNKS = 32
PERM_STRIP = 512
NEG_BIG = -1e30


def _cparams(sem):
    return pltpu.CompilerParams(dimension_semantics=sem, vmem_limit_bytes=VMEM_LIMIT)


def _silu(v):
    return v * jax.nn.sigmoid(v)


def _split3(v):
    a = v.astype(BF16)
    r = v - a.astype(F32)
    b = r.astype(BF16)
    c = (r - b.astype(F32)).astype(BF16)
    return a, b, c


def _dot01(m01, v):
    a, b, c = _split3(v)
    d = lambda t: jnp.dot(m01, t, preferred_element_type=F32)
    return d(a) + d(b) + d(c)


def _dot_split(v, v_hi, w_ref):
    v_lo = (v - v_hi.astype(F32)).astype(BF16)
    d = lambda a, b: jnp.dot(a, b, preferred_element_type=F32)
    return d(v_hi, w_ref[0]) + (d(v_hi, w_ref[1]) + d(v_lo, w_ref[0]))


def _split_hi_lo(w):
    hi = w.astype(BF16)
    return jnp.stack([hi, (w - hi.astype(F32)).astype(BF16)])


def _dot_nt(a, b):
    return lax.dot_general(a, b, (((1,), (1,)), ((), ())), preferred_element_type=F32)


def _dot_tn(a, b):
    return lax.dot_general(a, b, (((0,), (0,)), ((), ())), preferred_element_type=F32)


def _layer_norm(z, g, b):
    mu = jnp.mean(z, axis=-1, keepdims=True)
    zc = z - mu
    var = jnp.mean(zc * zc, axis=-1, keepdims=True)
    return zc * lax.rsqrt(var + LN_EPS) * g + b


def _mod_kernel(c_ref, w_ref, b_ref, o_ref):
    ca = _silu(c_ref[...])
    o_ref[0] = jnp.dot(ca, w_ref[0], precision=HIGHEST, preferred_element_type=F32) + b_ref[0]


def ada_mod(c, ada_w, ada_b):
    depth, d, d6 = ada_w.shape
    b = c.shape[0]
    tn = min(d6, 1536)
    return pl.pallas_call(
        _mod_kernel,
        grid=(depth, d6 // tn),
        in_specs=[pl.BlockSpec((b, d), lambda l, j: (0, 0)),
                  pl.BlockSpec((1, d, tn), lambda l, j: (l, 0, j)),
                  pl.BlockSpec((1, 1, tn), lambda l, j: (l, 0, j))],
        out_specs=pl.BlockSpec((1, b, tn), lambda l, j: (l, 0, j)),
        out_shape=jax.ShapeDtypeStruct((depth, b, d6), F32),
        compiler_params=_cparams(("parallel", "parallel")),
        name="ada_mod",
    )(c, ada_w, ada_b.reshape(depth, 1, d6))


def _hgrn_lower_bound(lg, layer):
    rows = [lg[i:i + 1, :] for i in range(lg.shape[0])]
    mx = functools.reduce(jnp.maximum, rows)
    ex = [jnp.exp(rw - mx) for rw in rows]
    den = functools.reduce(lambda a, b: a + b, ex)
    lb = jnp.zeros_like(mx)
    for i in range(1, layer + 1):
        lb = lb + ex[i] / den
    return jnp.clip(lb, 0.0, 1.0)


def _hgrn_consts():
    rr = lax.broadcasted_iota(jnp.int32, (HG_CHUNK, HG_CHUNK), 0)
    cc = lax.broadcasted_iota(jnp.int32, (HG_CHUNK, HG_CHUNK), 1)
    tril = jnp.where(cc <= rr, 1.0, 0.0).astype(BF16)
    sub8 = lax.broadcasted_iota(jnp.int32, (8, LANES), 0)
    lane8 = lax.broadcasted_iota(jnp.int32, (8, LANES), 1)
    return tril, sub8, lane8


def _hgrn_chunk(qs, sig, iv, gs, lb, nw, st_ref, c_ref, consts):
    C, SUB = HG_CHUNK, HG_SUB
    nsub = C // SUB
    tril, sub8, lane8 = consts
    one_m_lb = 1.0 - lb
    f = lb + one_m_lb * sig
    g = jnp.log2(jnp.maximum(f, LOG_FLOOR))
    k = one_m_lb * (1.0 - sig)
    b = _dot01(tril, g)
    r_end = [b[(j + 1) * SUB - 1:(j + 1) * SUB, :] for j in range(nsub)]
    bc = lambda v: jnp.broadcast_to(v, (SUB, LANES))
    r_prev = jnp.concatenate([jnp.zeros((SUB, LANES), F32)] + [bc(r_end[j]) for j in range(nsub - 1)], axis=0)
    r_own = jnp.concatenate([bc(r_end[j]) for j in range(nsub)], axis=0)
    r_last = r_end[nsub - 1]
    e_prev = jnp.concatenate([jnp.ones((SUB, LANES), F32)]
                             + [bc(jnp.exp2(r_end[j])) for j in range(nsub - 1)], axis=0)
    e_tail = jnp.concatenate([bc(jnp.exp2(r_last - r_end[j])) for j in range(nsub)], axis=0)
    qhat = qs * jnp.exp2(b - r_prev)
    khat = k * jnp.exp2(r_own - b)
    qtil = qhat * e_prev
    kst = khat * e_tail

    lhs, rhs = [], []
    for j in range(nsub - 1):
        lo = (j + 1) * SUB
        qj = qs[lo:, :] * jnp.exp2(b[lo:, :] - r_end[j])
        lhs.append(jnp.concatenate([jnp.zeros((lo, LANES), F32), qj], axis=0))
        parts = []
        if j > 0:
            parts.append(jnp.zeros((j * SUB, LANES), F32))
        parts.append(khat[j * SUB:(j + 1) * SUB, :])
        parts.append(jnp.zeros((C - (j + 1) * SUB, LANES), F32))
        rhs.append(jnp.concatenate(parts, axis=0))
    a_off = _dot_nt(jnp.concatenate(lhs, axis=1).astype(BF16),
                    jnp.concatenate(rhs, axis=1).astype(BF16))

    c_ref[...] = b - jnp.log2(k)
    pieces = []
    for blk in range(nsub):
        base = blk * SUB
        bt = [b[base:base + 8, :], b[base + 8:base + 16, :]]
        qt = [qs[base:base + 8, :], qs[base + 8:base + 16, :]]
        acc = [jnp.zeros((8, LANES), F32), jnp.zeros((8, LANES), F32)]
        for s in range(SUB):
            cs = jnp.broadcast_to(c_ref[base + s:base + s + 1, :], (8, LANES))
            for half in range(2):
                if s >= 8 and half == 0:
                    continue
                d = bt[half] - cs
                s_loc = s - 8 * half
                if s_loc >= 0:
                    d = jnp.where(sub8 >= s_loc, d, NEG_BIG)
                xval = jnp.exp2(d) * qt[half]
                col = jnp.sum(xval, axis=-1, keepdims=True)
                acc[half] = jnp.where(lane8 == base + s, col, acc[half])
        pieces += acc
    a_diag = jnp.concatenate(pieces, axis=0)
    a = (a_off + a_diag[:, :C]).astype(BF16)

    st = st_ref[...]
    o = jnp.dot(a, iv, preferred_element_type=F32) + _dot_nt(qtil.astype(BF16), st.astype(BF16))
    st_ref[...] = st * jnp.exp2(r_last) + _dot_tn(iv, kst.astype(BF16))

    ms = jnp.mean(o * o, axis=-1, keepdims=True)
    return o * lax.rsqrt(ms + RMS_EPS) * nw * gs


def _in_proj_kernel(x_ref, mod_ref, w_ref, waf_ref, sel_ref, hq_ref, hf_ref, hi_ref, hg_ref,
                    aq_ref, ak_ref, av_ref, af_ref, qq_ref, kk_ref, qk_ref, *, d):
    shift = mod_ref[0, 0:1, :]
    scale = mod_ref[0, 1:2, :]
    h = x_ref[...] * (1.0 + scale) + shift
    hb = h.astype(BF16)

    def proj(g):
        return jnp.dot(hb, w_ref[:, g * d:(g + 1) * d], preferred_element_type=F32)

    hq_ref[...] = _silu(proj(0)).astype(BF16)
    hf_ref[...] = jax.nn.sigmoid(proj(1))
    hi_ref[...] = proj(2).astype(BF16)
    hg_ref[...] = _silu(proj(3)).astype(BF16)
    qb = (proj(4) * (FOX_DH ** -0.5 * LOG2E)).astype(BF16)
    kb = proj(5).astype(BF16)
    aq_ref[...] = qb
    ak_ref[...] = kb
    av_ref[...] = proj(6).astype(BF16)
    af_ref[...] = _dot_split(h, hb, waf_ref)
    per_head = lambda v: jnp.dot(v, sel_ref[...], preferred_element_type=F32)
    qq_ref[...] = per_head(qb * qb)
    kk_ref[...] = per_head(kb * kb)
    qk_ref[...] = per_head(qb * kb)


def in_proj(x2, mod, w7, waf, seq):
    n, d = x2.shape
    tpb = seq // TM
    row = lambda i: (i, 0)
    big = lambda dt: jax.ShapeDtypeStruct((n, d), dt)
    small = jax.ShapeDtypeStruct((n, LANES), F32)
    sel = (jnp.arange(d)[:, None] // FOX_DH == jnp.arange(LANES)[None, :]).astype(BF16)
    return pl.pallas_call(
        functools.partial(_in_proj_kernel, d=d),
        grid=(n // TM,),
        in_specs=[pl.BlockSpec((TM, d), row),
                  pl.BlockSpec((1, 6, d), lambda i: (i // tpb, 0, 0)),
                  pl.BlockSpec((d, 7 * d), lambda i: (0, 0), pipeline_mode=pl.Buffered(1)),
                  pl.BlockSpec((2, d, LANES), lambda i: (0, 0, 0), pipeline_mode=pl.Buffered(1)),
                  pl.BlockSpec((d, LANES), lambda i: (0, 0), pipeline_mode=pl.Buffered(1))],
        out_specs=[pl.BlockSpec((TM, d), row)] * 7 + [pl.BlockSpec((TM, LANES), row)] * 4,
        out_shape=[big(BF16), big(F32), big(BF16), big(BF16), big(BF16), big(BF16), big(BF16),
                   small, small, small, small],
        compiler_params=_cparams(("parallel",)),
        name="in_proj",
    )(x2, mod, w7, waf, sel)


def _hgrn_kernel(q_ref, f_ref, i_ref, g_ref, lbl_ref, nw_ref, o_ref, st_ref, c_ref, *, layer, ts, hps):
    C = HG_CHUNK

    @pl.when(pl.program_id(2) == 0)
    def _():
        st_ref[...] = jnp.zeros_like(st_ref)

    lb_all = _hgrn_lower_bound(lbl_ref[...], layer)
    nw = nw_ref[...]
    consts = _hgrn_consts()

    def chunk(ci, carry):
        for u in range(HG_UNROLL):
            r0 = pl.multiple_of((ci * HG_UNROLL + u) * C, C)
            for hh in range(hps):
                sl = slice(hh * HG_DK, (hh + 1) * HG_DK)
                o = _hgrn_chunk(q_ref[pl.ds(r0, C), sl].astype(F32), f_ref[pl.ds(r0, C), sl],
                                i_ref[pl.ds(r0, C), sl], g_ref[pl.ds(r0, C), sl].astype(F32),
                                lb_all[:, sl], nw, st_ref.at[hh], c_ref.at[u * hps + hh], consts)
                o_ref[pl.ds(r0, C), sl] = o.astype(BF16)
        return carry

    lax.fori_loop(0, ts // (C * HG_UNROLL), chunk, 0)


def hgrn2(hq, hf, hi, hg, lb_logits, norm_w, *, layer, batch, seq):
    n, width = hq.shape
    hps = HG_HEADS_PER_STEP
    groups = width // (hps * HG_DK)
    ts = min(HG_TS, seq)
    spb = seq // ts
    depth = lb_logits.shape[0]
    blk = pl.BlockSpec((ts, hps * HG_DK), lambda b, h, t: (b * spb + t, h))
    return pl.pallas_call(
        functools.partial(_hgrn_kernel, layer=layer, ts=ts, hps=hps),
        grid=(batch, groups, spb),
        in_specs=[blk, blk, blk, blk,
                  pl.BlockSpec((depth, hps * HG_DK), lambda b, h, t: (0, h)),
                  pl.BlockSpec((1, HG_DK), lambda b, h, t: (0, 0))],
        out_specs=blk,
        out_shape=jax.ShapeDtypeStruct((n, width), BF16),
        scratch_shapes=[pltpu.VMEM((hps, HG_DK, HG_DK), F32), pltpu.VMEM((HG_UNROLL * hps, HG_CHUNK, HG_DK), F32)],
        compiler_params=_cparams(("parallel", "parallel", "arbitrary")),
        name="hgrn2",
    )(hq, hf, hi, hg, lb_logits, norm_w.reshape(1, HG_DK))


def _fox_decay_kernel(af_ref, bias_ref, qq_ref, kk_ref, qk_ref, nf_ref, st_ref, carry_ref, *, t, tk, heads):
    @pl.when(pl.program_id(1) == 0)
    def _():
        carry_ref[...] = jnp.zeros_like(carry_ref)

    z = af_ref[...] + bias_ref[...]
    ls = jnp.minimum(z, 0.0) - jnp.log(1.0 + jnp.exp(-jnp.abs(z)))
    rr = lax.broadcasted_iota(jnp.int32, (t, t), 0)
    cc = lax.broadcasted_iota(jnp.int32, (t, t), 1)
    tril = jnp.where(cc <= rr, 1.0, 0.0).astype(BF16)
    cs = _dot01(tril, ls) + carry_ref[...]
    carry_ref[...] = cs[t - 1:t, :]
    nf = cs * (-LOG2E)
    nf_ref[0, :, 0, :] = nf.T[:heads, :]

    amax = jnp.max(jnp.sqrt(qq_ref[...]), axis=0, keepdims=True)
    dmin = jnp.min(qk_ref[...] + nf, axis=0, keepdims=True)
    kn = jnp.sqrt(kk_ref[...])
    sub = t // tk
    kmax = [jnp.max(kn[s * tk:(s + 1) * tk, :], axis=0, keepdims=True) for s in range(sub)]
    ends = [nf[(s + 1) * tk - 1:(s + 1) * tk, :] for s in range(sub)]
    pad = jnp.zeros((8 - 2 - 2 * sub, LANES), F32)
    st_ref[0, 0] = jnp.concatenate([amax, dmin] + kmax + ends + [pad], axis=0)


def fox_decay(af, bias, qq, kk, qk, *, batch, seq, heads):
    t = min(FOX_T, seq)
    nblk = seq // t
    row = pl.BlockSpec((t, LANES), lambda b, i: (b * nblk + i, 0))
    return pl.pallas_call(
        functools.partial(_fox_decay_kernel, t=t, tk=min(FOX_TK, t), heads=heads),
        grid=(batch, nblk),
        in_specs=[row, pl.BlockSpec((1, LANES), lambda b, i: (0, 0)), row, row, row],
        out_specs=[pl.BlockSpec((1, heads, 1, t), lambda b, i: (b, 0, 0, i)),
                   pl.BlockSpec((1, 1, 8, LANES), lambda b, i: (b, i, 0, 0))],
        out_shape=[jax.ShapeDtypeStruct((batch, heads, 1, seq), F32),
                   jax.ShapeDtypeStruct((batch, nblk, 8, LANES), F32)],
        scratch_shapes=[pltpu.VMEM((1, LANES), F32)],
        compiler_params=_cparams(("parallel", "arbitrary")),
        name="fox_decay",
    )(af, jnp.pad(bias, (0, LANES - heads)).reshape(1, LANES), qq, kk, qk)


FOX_SKIP_LOG2 = 170.0


def fox_first_blocks(stats, *, heads, sub):
    nb, nblk = stats.shape[0], stats.shape[1]
    amax, dmin = stats[:, :, 0, :heads], stats[:, :, 1, :heads]
    kmax = stats[:, :, 2:2 + sub, :heads].reshape(nb, nblk * sub, heads)
    e_end = stats[:, :, 2 + sub:2 + 2 * sub, :heads].reshape(nb, nblk * sub, heads)
    ub = amax[:, :, None, :] * kmax[:, None, :, :] * 1.02 + 2.0 - dmin[:, :, None, :] + e_end[:, None, :, :]
    earlier = (jnp.arange(nblk) * sub)[None, :, None, None] > jnp.arange(nblk * sub)[None, None, :, None]
    skip = (ub < -FOX_SKIP_LOG2) & earlier
    first = jnp.sum(jnp.cumprod(skip.astype(jnp.int32), axis=2), axis=2)
    first = jnp.min(first.reshape(first.shape[0], nblk, heads // 2, 2), axis=-1)
    return first.transpose(0, 2, 1).reshape(-1).astype(jnp.int32)


def _fox_kernel(first_ref, q_ref, k_ref, v_ref, nf_ref, o_ref, vaug_ref, acc_ref, m_ref, za_ref, zb_ref,
                *, t, tk, seq):
    nq = seq // t
    sub = t // tk
    pair_id = pl.program_id(0) * pl.num_programs(1) + pl.program_id(1)

    vaug_ref[:, :LANES] = v_ref[...]
    vaug_ref[:, LANES:] = jnp.ones((seq, LANES), BF16)

    assert sub == 1
    half = t // 2
    lane = lax.broadcasted_iota(jnp.int32, (t, LANES), 1)
    tri = (lax.broadcasted_iota(jnp.int32, (half, half), 1)
           <= lax.broadcasted_iota(jnp.int32, (half, half), 0))

    def q_block(qi, carry):
        r0 = pl.multiple_of(qi * t, t)
        q = q_ref[pl.ds(r0, t), :]
        zero = jnp.zeros_like(q)
        qh = [jnp.where(lane < FOX_DH, q, zero), jnp.where(lane >= FOX_DH, q, zero)]
        for h in range(2):
            acc_ref[h] = jnp.zeros((t, 2 * LANES), F32)
            m_ref[h] = jnp.full((t, LANES), -jnp.inf, F32)

        def logits(j, z_ref):
            c0 = pl.multiple_of(j * t, t)
            kb = k_ref[pl.ds(c0, t), :]
            for h in range(2):
                z_ref[h] = _dot_nt(qh[h], kb) + nf_ref[0, h, :, pl.ds(c0, t)]

        def logits_causal(z_ref):
            kb = k_ref[pl.ds(r0, t), :]
            for h in range(2):
                z_ref[h, :half, :half] = _dot_nt(qh[h][:half], kb[:half]) + nf_ref[0, h, :, pl.ds(r0, half)]
                z_ref[h, half:, :] = _dot_nt(qh[h][half:], kb) + nf_ref[0, h, :, pl.ds(r0, t)]

        def update(h, rows, z, vb):
            m_old = m_ref[h, rows, :]
            m_new = jnp.maximum(m_old, jnp.max(z, axis=-1, keepdims=True))
            alpha = jnp.exp2(m_old - m_new)
            p = jnp.exp2((z - jnp.concatenate([m_new] * (z.shape[1] // LANES), axis=1)).astype(BF16))
            pv = jnp.dot(p, vb, preferred_element_type=F32)
            acc_ref[h, rows, :] = jnp.concatenate([alpha, alpha], axis=1) * acc_ref[h, rows, :] + pv
            m_ref[h, rows, :] = m_new

        def consume(j, z_ref):
            vb = vaug_ref[pl.ds(pl.multiple_of(j * t, t), t), :]
            for h in range(2):
                update(h, slice(0, t), z_ref[h], vb)

        def consume_causal(z_ref):
            for h in range(2):
                top = jnp.where(tri, z_ref[h, :half, :half], MASK_VALUE)
                update(h, slice(0, half), top, vaug_ref[pl.ds(r0, half), :])
                right = jnp.where(tri, z_ref[h, half:, half:], MASK_VALUE)
                bottom = jnp.concatenate([z_ref[h, half:, :half], right], axis=1)
                update(h, slice(half, t), bottom, vaug_ref[pl.ds(r0, t), :])

        start = jnp.minimum(first_ref[pair_id * nq + qi], qi)
        before = qi - start

        @pl.when(before == 0)
        def _():
            logits_causal(za_ref)
            consume_causal(za_ref)

        @pl.when(before > 0)
        def _():
            logits(start, za_ref)

            def pair(p, c):
                j = start + 2 * p
                logits(j + 1, zb_ref)
                consume(j, za_ref)
                logits(j + 2, za_ref)
                consume(j + 1, zb_ref)
                return c

            lax.fori_loop(0, (before - 1) // 2, pair, 0)
            last = qi - 1

            @pl.when(before % 2 == 1)
            def _():
                logits_causal(zb_ref)
                consume(last, za_ref)
                consume_causal(zb_ref)

            @pl.when(before % 2 == 0)
            def _():
                logits(last, zb_ref)
                consume(last - 1, za_ref)
                logits_causal(za_ref)
                consume(last, zb_ref)
                consume_causal(za_ref)

        a0 = acc_ref[0]
        a1 = acc_ref[1]
        o0 = a0[:, :LANES] / a0[:, LANES:]
        o1 = a1[:, :LANES] / a1[:, LANES:]
        o_ref[pl.ds(r0, t), :] = jnp.where(lane < FOX_DH, o0, o1).astype(BF16)
        return carry

    lax.fori_loop(0, nq, q_block, 0)


def fox_attn(first, aq, ak, av, negf, *, batch, seq):
    n, width = aq.shape
    pairs = width // LANES
    t = min(FOX_T, seq)
    tk = min(FOX_TK, t)
    whole = pl.BlockSpec((seq, LANES), lambda b, p, f: (b, p))
    grid_spec = pltpu.PrefetchScalarGridSpec(
        num_scalar_prefetch=1,
        grid=(batch, pairs),
        in_specs=[whole, whole, whole, pl.BlockSpec((1, 2, 1, seq), lambda b, p, f: (b, p, 0, 0))],
        out_specs=whole,
        scratch_shapes=[pltpu.VMEM((seq, 2 * LANES), BF16), pltpu.VMEM((2, t, 2 * LANES), F32),
                        pltpu.VMEM((2, t, LANES), F32), pltpu.VMEM((2, t, tk), F32), pltpu.VMEM((2, t, tk), F32)],
    )
    return pl.pallas_call(
        functools.partial(_fox_kernel, t=t, tk=tk, seq=seq),
        grid_spec=grid_spec,
        out_shape=jax.ShapeDtypeStruct((n, width), BF16),
        compiler_params=_cparams(("parallel", "parallel")),
        name="fox_attn",
    )(first, aq, ak, av, negf)


def _mixer_out_kernel(oa_ref, ob_ref, x_ref, mod_ref, wg_ref, wbr_ref, wo_ref, ln_ref, wr_ref, br_ref,
                      x1_ref, h2_ref, lg_ref, *, d, alpha):
    shift1, scale1, gate1 = mod_ref[0, 0:1, :], mod_ref[0, 1:2, :], mod_ref[0, 2:3, :]
    shift2, scale2 = mod_ref[0, 3:4, :], mod_ref[0, 4:5, :]
    x = x_ref[...]
    hb = (x * (1.0 + scale1) + shift1).astype(BF16)
    ga = jnp.dot(hb, wg_ref[:, :d], preferred_element_type=F32)
    gb = jnp.dot(hb, wg_ref[:, d:], preferred_element_type=F32)
    pa = jnp.dot(oa_ref[...], wbr_ref[:d, :], preferred_element_type=F32)
    pb = jnp.dot(ob_ref[...], wbr_ref[d:, :], preferred_element_type=F32)
    merged = jax.nn.sigmoid(ga) * pa + jax.nn.sigmoid(gb) * pb
    y = jnp.dot(merged.astype(BF16), wo_ref[...], preferred_element_type=F32)
    x1 = _layer_norm(alpha * x + (1.0 + gate1) * y, ln_ref[0:1, :], ln_ref[1:2, :])
    x1_ref[...] = x1
    h2 = x1 * (1.0 + scale2) + shift2
    h2b = h2.astype(BF16)
    h2_ref[...] = h2b
    lg_ref[...] = _dot_split(h2, h2b, wr_ref) + br_ref[...]


def mixer_out(oa, ob, x2, mod, wg, wbr, wo, ln, wr, br, *, seq, alpha):
    n, d = x2.shape
    tpb = seq // TM
    row = lambda i: (i, 0)
    const = lambda shape: pl.BlockSpec(shape, lambda i: (0, 0), pipeline_mode=pl.Buffered(1))
    return pl.pallas_call(
        functools.partial(_mixer_out_kernel, d=d, alpha=alpha),
        grid=(n // TM,),
        in_specs=[pl.BlockSpec((TM, d), row), pl.BlockSpec((TM, d), row), pl.BlockSpec((TM, d), row),
                  pl.BlockSpec((1, 6, d), lambda i: (i // tpb, 0, 0)),
                  const((d, 2 * d)), const((2 * d, d)), const((d, d)), const((2, d)),
                  pl.BlockSpec((2, d, LANES), lambda i: (0, 0, 0), pipeline_mode=pl.Buffered(1)),
                  const((1, LANES))],
        out_specs=[pl.BlockSpec((TM, d), row), pl.BlockSpec((TM, d), row), pl.BlockSpec((TM, LANES), row)],
        out_shape=[jax.ShapeDtypeStruct((n, d), F32), jax.ShapeDtypeStruct((n, d), BF16),
                   jax.ShapeDtypeStruct((n, LANES), F32)],
        compiler_params=_cparams(("parallel",)),
        name="mixer_out",
    )(oa, ob, x2, mod, wg, wbr, wo, ln, wr, br)


def _tile_rows(t):
    rows = t * TOP_K + N_EXPERTS * (CHUNK - 1)
    return -(-rows // (8 * CHUNK)) * (8 * CHUNK)


def _dispatch_kernel(lg_ref, h2_ref, xs_ref, ys0_ref, rw_ref, cnt_ref, *, t, rows, nt):
    ys0_ref[...] = jnp.zeros_like(ys0_ref)

    @pl.when(pl.program_id(0) < nt)
    def _():
        _route_tile(lg_ref, h2_ref, xs_ref, rw_ref, cnt_ref, t=t, rows=rows)


def _route_tile(lg_ref, h2_ref, xs_ref, rw_ref, cnt_ref, *, t, rows):
    lane = lax.broadcasted_iota(jnp.int32, (t, LANES), 1).astype(F32)
    lg = lg_ref[...]
    onehots, vals = [], []
    for _ in range(TOP_K):
        mx = jnp.max(lg, axis=-1, keepdims=True)
        idx = jnp.min(jnp.where(lg == mx, lane, float(LANES)), axis=-1, keepdims=True)
        oh = lane == idx
        onehots.append(oh)
        vals.append(mx)
        lg = jnp.where(oh, -jnp.inf, lg)
    es = [jnp.exp(v - vals[0]) for v in vals]
    den = functools.reduce(lambda a, b: a + b, es)
    ws = [e / den for e in es]

    member = functools.reduce(lambda a, b: a + b, [jnp.where(oh, 1.0, 0.0) for oh in onehots])
    rr = lax.broadcasted_iota(jnp.int32, (t, t), 0)
    cc = lax.broadcasted_iota(jnp.int32, (t, t), 1)
    strict_lower = jnp.where(cc < rr, 1.0, 0.0).astype(BF16)
    rank = jnp.dot(strict_lower, member.astype(BF16), preferred_element_type=F32)
    cnt = jnp.sum(member, axis=0, keepdims=True)
    nch = jnp.floor((cnt + (CHUNK - 1)) * (1.0 / CHUNK))
    ur = lax.broadcasted_iota(jnp.int32, (LANES, LANES), 0)
    uc = lax.broadcasted_iota(jnp.int32, (LANES, LANES), 1)
    strict_upper = jnp.where(ur < uc, 1.0, 0.0).astype(BF16)
    toff = jnp.dot(jnp.broadcast_to(nch, (8, LANES)).astype(BF16), strict_upper,
                   preferred_element_type=F32)[0:1, :]
    pos = toff * float(CHUNK) + rank

    rw = jnp.full((t, LANES), -1.0, F32)
    for kk in range(TOP_K):
        r_k = jnp.sum(jnp.where(onehots[kk], pos, 0.0), axis=-1, keepdims=True)
        rw = jnp.where(lane == kk, r_k, rw)
        rw = jnp.where(lane == TOP_K + kk, ws[kk], rw)
    rw_ref[...] = rw
    cnt_ref[0] = jnp.broadcast_to(cnt, (8, LANES))

    rt = rw.T
    h2 = h2_ref[...]
    for r0 in range(0, rows, PERM_STRIP):
        riota = (lax.broadcasted_iota(jnp.int32, (PERM_STRIP, t), 0) + r0).astype(F32)
        pm = jnp.zeros((PERM_STRIP, t), F32)
        for kk in range(TOP_K):
            pm = jnp.where(riota == rt[kk:kk + 1, :], 1.0, pm)
        xs_ref[r0:r0 + PERM_STRIP, :] = jnp.dot(pm.astype(BF16), h2, preferred_element_type=F32).astype(BF16)


def dispatch(logits, h2):
    n, d = h2.shape
    t = min(ROUTE_T, n)
    rows = _tile_rows(t)
    nt = n // t
    assert 2 * BLOCK_CHUNKS * CHUNK <= rows
    tile = lambda i: (jnp.minimum(i, nt - 1), 0)
    return pl.pallas_call(
        functools.partial(_dispatch_kernel, t=t, rows=rows, nt=nt),
        grid=(nt + 1,),
        in_specs=[pl.BlockSpec((t, LANES), tile), pl.BlockSpec((t, d), tile)],
        out_specs=[pl.BlockSpec((rows, d), tile), pl.BlockSpec((rows, d), lambda i: (i, 0)),
                   pl.BlockSpec((t, LANES), tile),
                   pl.BlockSpec((1, 8, LANES), lambda i: (jnp.minimum(i, nt - 1), 0, 0))],
        out_shape=[jax.ShapeDtypeStruct((nt * rows, d), BF16),
                   jax.ShapeDtypeStruct(((nt + 1) * rows, d), BF16),
                   jax.ShapeDtypeStruct((n, LANES), F32), jax.ShapeDtypeStruct((nt, 8, LANES), F32)],
        compiler_params=_cparams(("arbitrary",)),
        name="dispatch",
    )(logits, h2)


def _num_blocks(nt, t):
    max_chunks = nt * ((t * TOP_K + N_EXPERTS * (CHUNK - 1)) // CHUNK)
    return -(-max_chunks // BLOCK_CHUNKS) + N_EXPERTS


def _routing_tables(cnt, t):
    nt = cnt.shape[0]
    e = N_EXPERTS
    cpt = _tile_rows(t) // CHUNK
    nb = _num_blocks(nt, t)
    nch = (cnt + (CHUNK - 1)) // CHUNK
    toff = jnp.cumsum(nch, axis=1) - nch
    n_e = jnp.sum(nch, axis=0)
    nblk = (n_e + (BLOCK_CHUNKS - 1)) // BLOCK_CHUNKS
    bend = jnp.cumsum(nblk)
    bstart = bend - nblk
    total_blocks = bend[-1]
    cumj_incl = jnp.cumsum(nch, axis=0)
    cumj = cumj_incl - nch

    b = jnp.arange(nb, dtype=jnp.int32)
    last = jnp.maximum(total_blocks - 1, 0)
    bex = jnp.minimum(jnp.sum(bend[None, :] <= jnp.minimum(b, last)[:, None], axis=1), e - 1).astype(jnp.int32)
    bvalid = b < total_blocks
    sel = (bex[:, None] == jnp.arange(e, dtype=jnp.int32)[None, :]).astype(jnp.int32)
    row = lambda tab: jnp.sum(sel[:, :, None] * tab.T[None, :, :], axis=1)
    cum_b, excl_b, toff_b = row(cumj_incl), row(cumj), row(toff)
    n_b = jnp.sum(sel * n_e[None, :], axis=1)
    s0 = (b - jnp.sum(sel * bstart[None, :], axis=1)) * BLOCK_CHUNKS
    s = s0[:, None] + jnp.arange(BLOCK_CHUNKS, dtype=jnp.int32)[None, :]
    valid = bvalid[:, None] & (s < n_b[:, None])
    j = jnp.minimum(jnp.sum(cum_b[:, None, :] <= s[:, :, None], axis=2), nt - 1)
    jsel = (j[:, :, None] == jnp.arange(nt, dtype=jnp.int32)[None, None, :]).astype(jnp.int32)
    pick = lambda tab_b: jnp.sum(jsel * tab_b[:, None, :], axis=2)
    chunk = j * cpt + pick(toff_b) + (s - pick(excl_b))
    src = jnp.where(valid, chunk, 0).astype(jnp.int32).reshape(-1)
    spare = nt * cpt + (b[:, None] % 2) * BLOCK_CHUNKS + jnp.arange(BLOCK_CHUNKS, dtype=jnp.int32)[None, :]
    dst = jnp.where(valid, chunk, spare).astype(jnp.int32).reshape(-1)
    return bex, bvalid.astype(jnp.int32), src, dst, nb


def _expert_kernel(bex_ref, bvalid_ref, src_ref, dst_ref, xs_ref, wg_ref, bg_ref, wu_ref, bu_ref, wd_ref, bd_ref,
                   ys_in_ref, ys_ref, xbuf, ybuf, wbf, sem_in, sem, *, nb):
    del ys_in_ref
    i = pl.program_id(0)
    slot = i % 2

    def in_copy(step, sl, c):
        row = pl.multiple_of(src_ref[step * BLOCK_CHUNKS + c] * CHUNK, CHUNK)
        return pltpu.make_async_copy(xs_ref.at[pl.ds(row, CHUNK), :],
                                     xbuf.at[sl, pl.ds(c * CHUNK, CHUNK), :], sem_in.at[sl])

    def out_copy(step, sl, c):
        row = pl.multiple_of(dst_ref[step * BLOCK_CHUNKS + c] * CHUNK, CHUNK)
        return pltpu.make_async_copy(ybuf.at[sl, pl.ds(c * CHUNK, CHUNK), :],
                                     ys_ref.at[pl.ds(row, CHUNK), :], sem.at[sl])

    def gather(step, sl):
        @pl.when(bvalid_ref[step] == 1)
        def _():
            for c in range(BLOCK_CHUNKS):
                in_copy(step, sl, c).start()

    @pl.when(i == 0)
    def _():
        gather(0, 0)

    @pl.when(i + 1 < nb)
    def _():
        gather(jnp.minimum(i + 1, nb - 1), 1 - slot)

    def wait_step(step, sl):
        @pl.when(bvalid_ref[step] == 1)
        def _():
            for c in range(BLOCK_CHUNKS):
                out_copy(step, sl, c).wait()

    @pl.when(i >= 2)
    def _():
        wait_step(jnp.maximum(i - 2, 0), slot)

    prev = jnp.maximum(i - 1, 0)
    new_expert = (i == 0) | (bex_ref[i] != bex_ref[prev])

    @pl.when((bvalid_ref[i] == 1) & new_expert)
    def _():
        wbf[0] = wg_ref[0, 0].astype(BF16)
        wbf[1] = wu_ref[0, 0].astype(BF16)
        wbf[2] = wd_ref[0, 0].astype(BF16)

    @pl.when(bvalid_ref[i] == 1)
    def _():
        for c in range(BLOCK_CHUNKS):
            in_copy(i, slot, c).wait()
        xb = xbuf[slot]
        g = jnp.minimum(jnp.dot(xb, wbf[0], preferred_element_type=F32) + bg_ref[0, 0], SWIGLU_LIMIT)
        u = jnp.clip(jnp.dot(xb, wbf[1], preferred_element_type=F32) + bu_ref[0, 0], -SWIGLU_LIMIT, SWIGLU_LIMIT)
        a = (u + 1.0) * (g * jax.nn.sigmoid(SWIGLU_ALPHA * g))
        y = jnp.dot(a.astype(BF16), wbf[2], preferred_element_type=F32) + bd_ref[0, 0]
        ybuf[slot] = y.astype(BF16)
        for c in range(BLOCK_CHUNKS):
            out_copy(i, slot, c).start()

    @pl.when(i == nb - 1)
    def _():
        @pl.when(i >= 1)
        def _():
            wait_step(jnp.maximum(i - 1, 0), 1 - slot)
        wait_step(i, slot)


def experts(xs, ys_init, bex, bvalid, src, dst, w_gate, b_gate, w_up, b_up, w_down, b_down, *, nb, layer):
    d = xs.shape[1]
    depth, e, _, de = w_gate.shape
    rows = BLOCK_CHUNKS * CHUNK

    wspec = lambda shape: pl.BlockSpec((1,) + shape, lambda i, be, bv, sc, dc: (layer, be[i], 0, 0))
    hbm = pl.BlockSpec(memory_space=pl.ANY)
    grid_spec = pltpu.PrefetchScalarGridSpec(
        num_scalar_prefetch=4,
        grid=(nb,),
        in_specs=[hbm, wspec((1, d, de)), wspec((1, 1, de)), wspec((1, d, de)), wspec((1, 1, de)),
                  wspec((1, de, d)), wspec((1, 1, d)), hbm],
        out_specs=hbm,
        scratch_shapes=[pltpu.VMEM((2, rows, d), BF16), pltpu.VMEM((2, rows, d), BF16),
                        pltpu.VMEM((3, d, de), BF16), pltpu.SemaphoreType.DMA((2,)),
                        pltpu.SemaphoreType.DMA((2,))],
    )
    n_in = 4 + 1 + 6
    return pl.pallas_call(
        functools.partial(_expert_kernel, nb=nb),
        grid_spec=grid_spec,
        out_shape=jax.ShapeDtypeStruct(ys_init.shape, BF16),
        input_output_aliases={n_in: 0},
        compiler_params=_cparams(("arbitrary",)),
        name="experts",
    )(bex, bvalid, src, dst, xs,
      w_gate, b_gate.reshape(depth, e, 1, de), w_up, b_up.reshape(depth, e, 1, de),
      w_down, b_down.reshape(depth, e, 1, d), ys_init)


def _combine_kernel(ys_ref, rw_ref, x1_ref, mod_ref, ln_ref, o_ref, *, t, rows, alpha):
    gate2 = mod_ref[0, 5:6, :]
    rw = rw_ref[...]
    y2 = jnp.zeros((t, x1_ref.shape[1]), F32)
    for c0 in range(0, rows, PERM_STRIP):
        liota = (lax.broadcasted_iota(jnp.int32, (t, PERM_STRIP), 1) + c0).astype(F32)
        wm = jnp.zeros((t, PERM_STRIP), F32)
        for kk in range(TOP_K):
            wm = jnp.where(liota == rw[:, kk:kk + 1], rw[:, TOP_K + kk:TOP_K + kk + 1], wm)
        y2 = y2 + jnp.dot(wm.astype(BF16), ys_ref[c0:c0 + PERM_STRIP, :], preferred_element_type=F32)
    o_ref[...] = _layer_norm(alpha * x1_ref[...] + (1.0 + gate2) * y2, ln_ref[0:1, :], ln_ref[1:2, :])


def combine(ys, rw, x1, mod, ln, *, seq, alpha):
    n, d = x1.shape
    t = min(ROUTE_T, n)
    rows = _tile_rows(t)
    tpb = max(seq // t, 1)
    return pl.pallas_call(
        functools.partial(_combine_kernel, t=t, rows=rows, alpha=alpha),
        grid=(n // t,),
        in_specs=[pl.BlockSpec((rows, d), lambda i: (i, 0)),
                  pl.BlockSpec((t, LANES), lambda i: (i, 0)),
                  pl.BlockSpec((t, d), lambda i: (i, 0)),
                  pl.BlockSpec((1, 6, d), lambda i: (i // tpb, 0, 0)),
                  pl.BlockSpec((2, d), lambda i: (0, 0))],
        out_specs=pl.BlockSpec((t, d), lambda i: (i, 0)),
        out_shape=jax.ShapeDtypeStruct((n, d), F32),
        compiler_params=_cparams(("parallel",)),
        name="combine",
    )(ys, rw, x1, mod, ln)


def moe_grouped(logits, h2, w_gate, b_gate, w_up, b_up, w_down, b_down, *, layer):
    n, d = h2.shape
    t = min(ROUTE_T, n)
    xs, ys_init, rw, cnt = dispatch(logits, h2)
    cnt_i = cnt[:, 0, :N_EXPERTS].astype(jnp.int32)
    bex, bvalid, src, dst, nb = _routing_tables(cnt_i, t)
    ys = experts(xs, ys_init, bex, bvalid, src, dst, w_gate, b_gate, w_up, b_up, w_down, b_down,
                 nb=nb, layer=layer)
    return ys, rw


def kernel(x, c, w_in, fox_f_bias, hg_lb_logits, hg_norm_w, w_branch, w_out, ada_w, ada_b,
           ln1_g, ln1_b, w_router, b_router, w_gate, b_gate, w_up, b_up, w_down, b_down,
           ln2_g, ln2_b):
    batch, seq, d = x.shape
    depth = w_in.shape[0]
    n = batch * seq
    alpha = (2 * depth) ** 0.25
    hgw = HG_HEADS * HG_DK
    foxw = FOX_HEADS * FOX_DH
    off_af = 4 * hgw + 3 * foxw
    off_g = off_af + FOX_HEADS

    mod_all = ada_mod(c, ada_w, ada_b).reshape(depth, batch, 6, d)
    x2 = x.reshape(n, d)

    for l in range(depth):
        mod = mod_all[l]
        w7 = w_in[l, :, :off_af].astype(BF16)
        waf = _split_hi_lo(jnp.pad(w_in[l, :, off_af:off_g], ((0, 0), (0, LANES - FOX_HEADS))))
        wg = w_in[l, :, off_g:].astype(BF16)
        hq, hf, hi, hg, aq, ak, av, af, qq, kk, qk = in_proj(x2, mod, w7, waf, seq)

        negf, stats = fox_decay(af, fox_f_bias[l], qq, kk, qk, batch=batch, seq=seq, heads=FOX_HEADS)
        first = fox_first_blocks(stats, heads=FOX_HEADS, sub=min(FOX_T, seq) // min(FOX_TK, seq))
        o_a = hgrn2(hq, hf, hi, hg, hg_lb_logits, hg_norm_w[l], layer=l, batch=batch, seq=seq)
        o_b = fox_attn(first, aq, ak, av, negf, batch=batch, seq=seq)

        wr = _split_hi_lo(jnp.pad(w_router[l], ((0, 0), (0, LANES - N_EXPERTS))))
        br = jnp.pad(b_router[l], (0, LANES - N_EXPERTS), constant_values=NEG_BIG).reshape(1, LANES)
        x1, h2, logits = mixer_out(
            o_a, o_b, x2, mod, wg, w_branch[l].astype(BF16), w_out[l].astype(BF16),
            jnp.stack([ln1_g[l], ln1_b[l]]), wr, br, seq=seq, alpha=alpha)

        ys, rw = moe_grouped(logits, h2, w_gate, b_gate, w_up, b_up, w_down, b_down, layer=l)
        x2 = combine(ys, rw, x1, mod, jnp.stack([ln2_g[l], ln2_b[l]]), seq=seq, alpha=alpha)
    return x2.reshape(batch, seq, d)
```

```python
import functools
import math

import jax
import jax.numpy as jnp
from jax import lax
from jax.experimental import pallas as pl
from jax.experimental.pallas import tpu as pltpu

F32 = jnp.float32
BF16 = jnp.bfloat16
HIGHEST = lax.Precision.HIGHEST

HG_HEADS = 8
HG_DK = 128
FOX_HEADS = 16
FOX_DH = 64
N_EXPERTS = 32
TOP_K = 4
LOG_FLOOR = 1e-30
MASK_VALUE = -1e30
SWIGLU_LIMIT = 7.0
SWIGLU_ALPHA = 1.702
LN_EPS = 1e-5
RMS_EPS = 1e-6
LOG2E = math.log2(math.e)

LANES = 128
BF16_SUBLANES = 16
VMEM_LIMIT = 56 * 1024 * 1024

HG_CHUNK = 64
HG_SUB = 16
HG_TS = 1024
HG_UNROLL = 4
HG_HEADS_PER_STEP = 8
FOX_T = 512
FOX_TK = 512
TM = 512
ROUTE_T = 512
CHUNK = BF16_SUBLANES
BLOCK_CHUNKS = 32
PERM_STRIP = 512
NEG_BIG = -1e30


def _cparams(sem):
    return pltpu.CompilerParams(dimension_semantics=sem, vmem_limit_bytes=VMEM_LIMIT)


def _silu(v):
    return v * jax.nn.sigmoid(v)


def _split3(v):
    a = v.astype(BF16)
    r = v - a.astype(F32)
    b = r.astype(BF16)
    c = (r - b.astype(F32)).astype(BF16)
    return a, b, c


def _dot01(m01, v):
    a, b, c = _split3(v)
    d = lambda t: jnp.dot(m01, t, preferred_element_type=F32)
    return d(a) + d(b) + d(c)


def _dot_split(v, v_hi, w_ref):
    v_lo = (v - v_hi.astype(F32)).astype(BF16)
    d = lambda a, b: jnp.dot(a, b, preferred_element_type=F32)
    return d(v_hi, w_ref[0]) + (d(v_hi, w_ref[1]) + d(v_lo, w_ref[0]))


def _split_hi_lo(w):
    hi = w.astype(BF16)
    return jnp.stack([hi, (w - hi.astype(F32)).astype(BF16)])


def _dot_nt(a, b):
    return lax.dot_general(a, b, (((1,), (1,)), ((), ())), preferred_element_type=F32)


def _dot_tn(a, b):
    return lax.dot_general(a, b, (((0,), (0,)), ((), ())), preferred_element_type=F32)


def _layer_norm(z, g, b):
    mu = jnp.mean(z, axis=-1, keepdims=True)
    zc = z - mu
    var = jnp.mean(zc * zc, axis=-1, keepdims=True)
    return zc * lax.rsqrt(var + LN_EPS) * g + b


def _mod_kernel(c_ref, w_ref, b_ref, o_ref):
    ca = _silu(c_ref[...])
    o_ref[0] = jnp.dot(ca, w_ref[0], precision=HIGHEST, preferred_element_type=F32) + b_ref[0]


def ada_mod(c, ada_w, ada_b):
    depth, d, d6 = ada_w.shape
    b = c.shape[0]
    tn = min(d6, 1536)
    return pl.pallas_call(
        _mod_kernel,
        grid=(depth, d6 // tn),
        in_specs=[pl.BlockSpec((b, d), lambda l, j: (0, 0)),
                  pl.BlockSpec((1, d, tn), lambda l, j: (l, 0, j)),
                  pl.BlockSpec((1, 1, tn), lambda l, j: (l, 0, j))],
        out_specs=pl.BlockSpec((1, b, tn), lambda l, j: (l, 0, j)),
        out_shape=jax.ShapeDtypeStruct((depth, b, d6), F32),
        compiler_params=_cparams(("parallel", "parallel")),
        name="ada_mod",
    )(c, ada_w, ada_b.reshape(depth, 1, d6))


def _hgrn_lower_bound(lg, layer):
    rows = [lg[i:i + 1, :] for i in range(lg.shape[0])]
    mx = functools.reduce(jnp.maximum, rows)
    ex = [jnp.exp(rw - mx) for rw in rows]
    den = functools.reduce(lambda a, b: a + b, ex)
    lb = jnp.zeros_like(mx)
    for i in range(1, layer + 1):
        lb = lb + ex[i] / den
    return jnp.clip(lb, 0.0, 1.0)


def _hgrn_consts():
    rr = lax.broadcasted_iota(jnp.int32, (HG_CHUNK, HG_CHUNK), 0)
    cc = lax.broadcasted_iota(jnp.int32, (HG_CHUNK, HG_CHUNK), 1)
    tril = jnp.where(cc <= rr, 1.0, 0.0).astype(BF16)
    sub8 = lax.broadcasted_iota(jnp.int32, (8, LANES), 0)
    lane8 = lax.broadcasted_iota(jnp.int32, (8, LANES), 1)
    return tril, sub8, lane8


def _hgrn_chunk(qs, sig, iv, gs, lb, nw, st_ref, c_ref, consts):
    C, SUB = HG_CHUNK, HG_SUB
    nsub = C // SUB
    tril, sub8, lane8 = consts
    one_m_lb = 1.0 - lb
    f = lb + one_m_lb * sig
    g = jnp.log2(jnp.maximum(f, LOG_FLOOR))
    k = one_m_lb * (1.0 - sig)
    b = _dot01(tril, g)
    r_end = [b[(j + 1) * SUB - 1:(j + 1) * SUB, :] for j in range(nsub)]
    bc = lambda v: jnp.broadcast_to(v, (SUB, LANES))
    r_prev = jnp.concatenate([jnp.zeros((SUB, LANES), F32)] + [bc(r_end[j]) for j in range(nsub - 1)], axis=0)
    r_own = jnp.concatenate([bc(r_end[j]) for j in range(nsub)], axis=0)
    r_last = r_end[nsub - 1]
    e_prev = jnp.concatenate([jnp.ones((SUB, LANES), F32)]
                             + [bc(jnp.exp2(r_end[j])) for j in range(nsub - 1)], axis=0)
    e_tail = jnp.concatenate([bc(jnp.exp2(r_last - r_end[j])) for j in range(nsub)], axis=0)
    qhat = qs * jnp.exp2(b - r_prev)
    khat = k * jnp.exp2(r_own - b)
    qtil = qhat * e_prev
    kst = khat * e_tail

    lhs, rhs = [], []
    for j in range(nsub - 1):
        lo = (j + 1) * SUB
        qj = qs[lo:, :] * jnp.exp2(b[lo:, :] - r_end[j])
        lhs.append(jnp.concatenate([jnp.zeros((lo, LANES), F32), qj], axis=0))
        parts = []
        if j > 0:
            parts.append(jnp.zeros((j * SUB, LANES), F32))
        parts.append(khat[j * SUB:(j + 1) * SUB, :])
        parts.append(jnp.zeros((C - (j + 1) * SUB, LANES), F32))
        rhs.append(jnp.concatenate(parts, axis=0))
    a_off = _dot_nt(jnp.concatenate(lhs, axis=1).astype(BF16),
                    jnp.concatenate(rhs, axis=1).astype(BF16))

    c_ref[...] = b - jnp.log2(k)
    pieces = []
    for blk in range(nsub):
        base = blk * SUB
        bt = [b[base:base + 8, :], b[base + 8:base + 16, :]]
        qt = [qs[base:base + 8, :], qs[base + 8:base + 16, :]]
        acc = [jnp.zeros((8, LANES), F32), jnp.zeros((8, LANES), F32)]
        for s in range(SUB):
            cs = jnp.broadcast_to(c_ref[base + s:base + s + 1, :], (8, LANES))
            for half in range(2):
                if s >= 8 and half == 0:
                    continue
                d = bt[half] - cs
                s_loc = s - 8 * half
                if s_loc >= 0:
                    d = jnp.where(sub8 >= s_loc, d, NEG_BIG)
                xval = jnp.exp2(d) * qt[half]
                col = jnp.sum(xval, axis=-1, keepdims=True)
                acc[half] = jnp.where(lane8 == base + s, col, acc[half])
        pieces += acc
    a_diag = jnp.concatenate(pieces, axis=0)
    a = (a_off + a_diag[:, :C]).astype(BF16)

    st = st_ref[...]
    o = jnp.dot(a, iv, preferred_element_type=F32) + _dot_nt(qtil.astype(BF16), st.astype(BF16))
    st_ref[...] = st * jnp.exp2(r_last) + _dot_tn(iv, kst.astype(BF16))

    ms = jnp.mean(o * o, axis=-1, keepdims=True)
    return o * lax.rsqrt(ms + RMS_EPS) * nw * gs


def _in_proj_kernel(x_ref, mod_ref, w_ref, waf_ref, sel_ref, hq_ref, hf_ref, hi_ref, hg_ref,
                    aq_ref, ak_ref, av_ref, af_ref, qq_ref, kk_ref, qk_ref, *, d):
    shift = mod_ref[0, 0:1, :]
    scale = mod_ref[0, 1:2, :]
    h = x_ref[...] * (1.0 + scale) + shift
    hb = h.astype(BF16)

    def proj(g):
        return jnp.dot(hb, w_ref[:, g * d:(g + 1) * d], preferred_element_type=F32)

    hq_ref[...] = _silu(proj(0)).astype(BF16)
    hf_ref[...] = jax.nn.sigmoid(proj(1))
    hi_ref[...] = proj(2).astype(BF16)
    hg_ref[...] = _silu(proj(3)).astype(BF16)
    qb = (proj(4) * (FOX_DH ** -0.5 * LOG2E)).astype(BF16)
    kb = proj(5).astype(BF16)
    aq_ref[...] = qb
    ak_ref[...] = kb
    av_ref[...] = proj(6).astype(BF16)
    af_ref[...] = _dot_split(h, hb, waf_ref)
    per_head = lambda v: jnp.dot(v, sel_ref[...], preferred_element_type=F32)
    qq_ref[...] = per_head(qb * qb)
    kk_ref[...] = per_head(kb * kb)
    qk_ref[...] = per_head(qb * kb)


def in_proj(x2, mod, w7, waf, seq):
    n, d = x2.shape
    tpb = seq // TM
    row = lambda i: (i, 0)
    big = lambda dt: jax.ShapeDtypeStruct((n, d), dt)
    small = jax.ShapeDtypeStruct((n, LANES), F32)
    sel = (jnp.arange(d)[:, None] // FOX_DH == jnp.arange(LANES)[None, :]).astype(BF16)
    return pl.pallas_call(
        functools.partial(_in_proj_kernel, d=d),
        grid=(n // TM,),
        in_specs=[pl.BlockSpec((TM, d), row),
                  pl.BlockSpec((1, 6, d), lambda i: (i // tpb, 0, 0)),
                  pl.BlockSpec((d, 7 * d), lambda i: (0, 0), pipeline_mode=pl.Buffered(1)),
                  pl.BlockSpec((2, d, LANES), lambda i: (0, 0, 0), pipeline_mode=pl.Buffered(1)),
                  pl.BlockSpec((d, LANES), lambda i: (0, 0), pipeline_mode=pl.Buffered(1))],
        out_specs=[pl.BlockSpec((TM, d), row)] * 7 + [pl.BlockSpec((TM, LANES), row)] * 4,
        out_shape=[big(BF16), big(F32), big(BF16), big(BF16), big(BF16), big(BF16), big(BF16),
                   small, small, small, small],
        compiler_params=_cparams(("parallel",)),
        name="in_proj",
    )(x2, mod, w7, waf, sel)


def _hgrn_kernel(q_ref, f_ref, i_ref, g_ref, lbl_ref, nw_ref, o_ref, st_ref, c_ref, *, layer, ts, hps):
    C = HG_CHUNK

    @pl.when(pl.program_id(2) == 0)
    def _():
        st_ref[...] = jnp.zeros_like(st_ref)

    lb_all = _hgrn_lower_bound(lbl_ref[...], layer)
    nw = nw_ref[...]
    consts = _hgrn_consts()

    def chunk(ci, carry):
        for u in range(HG_UNROLL):
            r0 = pl.multiple_of((ci * HG_UNROLL + u) * C, C)
            for hh in range(hps):
                sl = slice(hh * HG_DK, (hh + 1) * HG_DK)
                o = _hgrn_chunk(q_ref[pl.ds(r0, C), sl].astype(F32), f_ref[pl.ds(r0, C), sl],
                                i_ref[pl.ds(r0, C), sl], g_ref[pl.ds(r0, C), sl].astype(F32),
                                lb_all[:, sl], nw, st_ref.at[hh], c_ref.at[u * hps + hh], consts)
                o_ref[pl.ds(r0, C), sl] = o.astype(BF16)
        return carry

    lax.fori_loop(0, ts // (C * HG_UNROLL), chunk, 0)


def hgrn2(hq, hf, hi, hg, lb_logits, norm_w, *, layer, batch, seq):
    n, width = hq.shape
    hps = HG_HEADS_PER_STEP
    groups = width // (hps * HG_DK)
    ts = min(HG_TS, seq)
    spb = seq // ts
    depth = lb_logits.shape[0]
    blk = pl.BlockSpec((ts, hps * HG_DK), lambda b, h, t: (b * spb + t, h))
    return pl.pallas_call(
        functools.partial(_hgrn_kernel, layer=layer, ts=ts, hps=hps),
        grid=(batch, groups, spb),
        in_specs=[blk, blk, blk, blk,
                  pl.BlockSpec((depth, hps * HG_DK), lambda b, h, t: (0, h)),
                  pl.BlockSpec((1, HG_DK), lambda b, h, t: (0, 0))],
        out_specs=blk,
        out_shape=jax.ShapeDtypeStruct((n, width), BF16),
        scratch_shapes=[pltpu.VMEM((hps, HG_DK, HG_DK), F32), pltpu.VMEM((HG_UNROLL * hps, HG_CHUNK, HG_DK), F32)],
        compiler_params=_cparams(("parallel", "parallel", "arbitrary")),
        name="hgrn2",
    )(hq, hf, hi, hg, lb_logits, norm_w.reshape(1, HG_DK))


def _fox_decay_kernel(af_ref, bias_ref, qq_ref, kk_ref, qk_ref, nf_ref, st_ref, carry_ref, *, t, tk, heads):
    @pl.when(pl.program_id(1) == 0)
    def _():
        carry_ref[...] = jnp.zeros_like(carry_ref)

    z = af_ref[...] + bias_ref[...]
    ls = jnp.minimum(z, 0.0) - jnp.log(1.0 + jnp.exp(-jnp.abs(z)))
    rr = lax.broadcasted_iota(jnp.int32, (t, t), 0)
    cc = lax.broadcasted_iota(jnp.int32, (t, t), 1)
    tril = jnp.where(cc <= rr, 1.0, 0.0).astype(BF16)
    cs = _dot01(tril, ls) + carry_ref[...]
    carry_ref[...] = cs[t - 1:t, :]
    nf = cs * (-LOG2E)
    nf_ref[0, :, 0, :] = nf.T[:heads, :]

    amax = jnp.max(jnp.sqrt(qq_ref[...]), axis=0, keepdims=True)
    dmin = jnp.min(qk_ref[...] + nf, axis=0, keepdims=True)
    kn = jnp.sqrt(kk_ref[...])
    sub = t // tk
    kmax = [jnp.max(kn[s * tk:(s + 1) * tk, :], axis=0, keepdims=True) for s in range(sub)]
    ends = [nf[(s + 1) * tk - 1:(s + 1) * tk, :] for s in range(sub)]
    pad = jnp.zeros((8 - 2 - 2 * sub, LANES), F32)
    st_ref[0, 0] = jnp.concatenate([amax, dmin] + kmax + ends + [pad], axis=0)


def fox_decay(af, bias, qq, kk, qk, *, batch, seq, heads):
    t = min(FOX_T, seq)
    nblk = seq // t
    row = pl.BlockSpec((t, LANES), lambda b, i: (b * nblk + i, 0))
    return pl.pallas_call(
        functools.partial(_fox_decay_kernel, t=t, tk=min(FOX_TK, t), heads=heads),
        grid=(batch, nblk),
        in_specs=[row, pl.BlockSpec((1, LANES), lambda b, i: (0, 0)), row, row, row],
        out_specs=[pl.BlockSpec((1, heads, 1, t), lambda b, i: (b, 0, 0, i)),
                   pl.BlockSpec((1, 1, 8, LANES), lambda b, i: (b, i, 0, 0))],
        out_shape=[jax.ShapeDtypeStruct((batch, heads, 1, seq), F32),
                   jax.ShapeDtypeStruct((batch, nblk, 8, LANES), F32)],
        scratch_shapes=[pltpu.VMEM((1, LANES), F32)],
        compiler_params=_cparams(("parallel", "arbitrary")),
        name="fox_decay",
    )(af, jnp.pad(bias, (0, LANES - heads)).reshape(1, LANES), qq, kk, qk)


FOX_SKIP_LOG2 = 170.0


def fox_first_blocks(stats, *, heads, sub):
    nb, nblk = stats.shape[0], stats.shape[1]
    amax, dmin = stats[:, :, 0, :heads], stats[:, :, 1, :heads]
    kmax = stats[:, :, 2:2 + sub, :heads].reshape(nb, nblk * sub, heads)
    e_end = stats[:, :, 2 + sub:2 + 2 * sub, :heads].reshape(nb, nblk * sub, heads)
    ub = amax[:, :, None, :] * kmax[:, None, :, :] * 1.02 + 2.0 - dmin[:, :, None, :] + e_end[:, None, :, :]
    earlier = (jnp.arange(nblk) * sub)[None, :, None, None] > jnp.arange(nblk * sub)[None, None, :, None]
    skip = (ub < -FOX_SKIP_LOG2) & earlier
    first = jnp.sum(jnp.cumprod(skip.astype(jnp.int32), axis=2), axis=2)
    first = jnp.min(first.reshape(first.shape[0], nblk, heads // 2, 2), axis=-1)
    return first.transpose(0, 2, 1).reshape(-1).astype(jnp.int32)


def _fox_kernel(first_ref, q_ref, k_ref, v_ref, nf_ref, o_ref, vaug_ref, acc_ref, m_ref, za_ref, zb_ref,
                *, t, tk, seq):
    nq = seq // t
    sub = t // tk
    pair_id = pl.program_id(0) * pl.num_programs(1) + pl.program_id(1)

    vaug_ref[:, :LANES] = v_ref[...]
    vaug_ref[:, LANES:] = jnp.ones((seq, LANES), BF16)

    assert sub == 1
    half = t // 2
    lane = lax.broadcasted_iota(jnp.int32, (t, LANES), 1)
    tri = (lax.broadcasted_iota(jnp.int32, (half, half), 1)
           <= lax.broadcasted_iota(jnp.int32, (half, half), 0))

    def q_block(qi, carry):
        r0 = pl.multiple_of(qi * t, t)
        q = q_ref[pl.ds(r0, t), :]
        zero = jnp.zeros_like(q)
        qh = [jnp.where(lane < FOX_DH, q, zero), jnp.where(lane >= FOX_DH, q, zero)]
        for h in range(2):
            acc_ref[h] = jnp.zeros((t, 2 * LANES), F32)
            m_ref[h] = jnp.full((t, LANES), -jnp.inf, F32)

        def logits(j, z_ref):
            c0 = pl.multiple_of(j * t, t)
            kb = k_ref[pl.ds(c0, t), :]
            for h in range(2):
                z_ref[h] = _dot_nt(qh[h], kb) + nf_ref[0, h, :, pl.ds(c0, t)]

        def logits_causal(z_ref):
            kb = k_ref[pl.ds(r0, t), :]
            for h in range(2):
                z_ref[h, :half, :half] = _dot_nt(qh[h][:half], kb[:half]) + nf_ref[0, h, :, pl.ds(r0, half)]
                z_ref[h, half:, :] = _dot_nt(qh[h][half:], kb) + nf_ref[0, h, :, pl.ds(r0, t)]

        def update(h, rows, z, vb):
            m_old = m_ref[h, rows, :]
            m_new = jnp.maximum(m_old, jnp.max(z, axis=-1, keepdims=True))
            alpha = jnp.exp2(m_old - m_new)
            p = jnp.exp2((z - jnp.concatenate([m_new] * (z.shape[1] // LANES), axis=1)).astype(BF16))
            pv = jnp.dot(p, vb, preferred_element_type=F32)
            acc_ref[h, rows, :] = jnp.concatenate([alpha, alpha], axis=1) * acc_ref[h, rows, :] + pv
            m_ref[h, rows, :] = m_new

        def consume(j, z_ref):
            vb = vaug_ref[pl.ds(pl.multiple_of(j * t, t), t), :]
            for h in range(2):
                update(h, slice(0, t), z_ref[h], vb)

        def consume_causal(z_ref):
            for h in range(2):
                top = jnp.where(tri, z_ref[h, :half, :half], MASK_VALUE)
                update(h, slice(0, half), top, vaug_ref[pl.ds(r0, half), :])
                right = jnp.where(tri, z_ref[h, half:, half:], MASK_VALUE)
                bottom = jnp.concatenate([z_ref[h, half:, :half], right], axis=1)
                update(h, slice(half, t), bottom, vaug_ref[pl.ds(r0, t), :])

        start = jnp.minimum(first_ref[pair_id * nq + qi], qi)
        before = qi - start

        @pl.when(before == 0)
        def _():
            logits_causal(za_ref)
            consume_causal(za_ref)

        @pl.when(before > 0)
        def _():
            logits(start, za_ref)

            def pair(p, c):
                j = start + 2 * p
                logits(j + 1, zb_ref)
                consume(j, za_ref)
                logits(j + 2, za_ref)
                consume(j + 1, zb_ref)
                return c

            lax.fori_loop(0, (before - 1) // 2, pair, 0)
            last = qi - 1

            @pl.when(before % 2 == 1)
            def _():
                logits_causal(zb_ref)
                consume(last, za_ref)
                consume_causal(zb_ref)

            @pl.when(before % 2 == 0)
            def _():
                logits(last, zb_ref)
                consume(last - 1, za_ref)
                logits_causal(za_ref)
                consume(last, zb_ref)
                consume_causal(za_ref)

        a0 = acc_ref[0]
        a1 = acc_ref[1]
        o0 = a0[:, :LANES] / a0[:, LANES:]
        o1 = a1[:, :LANES] / a1[:, LANES:]
        o_ref[pl.ds(r0, t), :] = jnp.where(lane < FOX_DH, o0, o1).astype(BF16)
        return carry

    lax.fori_loop(0, nq, q_block, 0)


def fox_attn(first, aq, ak, av, negf, *, batch, seq):
    n, width = aq.shape
    pairs = width // LANES
    t = min(FOX_T, seq)
    tk = min(FOX_TK, t)
    whole = pl.BlockSpec((seq, LANES), lambda b, p, f: (b, p))
    grid_spec = pltpu.PrefetchScalarGridSpec(
        num_scalar_prefetch=1,
        grid=(batch, pairs),
        in_specs=[whole, whole, whole, pl.BlockSpec((1, 2, 1, seq), lambda b, p, f: (b, p, 0, 0))],
        out_specs=whole,
        scratch_shapes=[pltpu.VMEM((seq, 2 * LANES), BF16), pltpu.VMEM((2, t, 2 * LANES), F32),
                        pltpu.VMEM((2, t, LANES), F32), pltpu.VMEM((2, t, tk), F32), pltpu.VMEM((2, t, tk), F32)],
    )
    return pl.pallas_call(
        functools.partial(_fox_kernel, t=t, tk=tk, seq=seq),
        grid_spec=grid_spec,
        out_shape=jax.ShapeDtypeStruct((n, width), BF16),
        compiler_params=_cparams(("parallel", "parallel")),
        name="fox_attn",
    )(first, aq, ak, av, negf)


def _mixer_out_kernel(oa_ref, ob_ref, x_ref, mod_ref, wg_ref, wbr_ref, wo_ref, ln_ref, wr_ref, br_ref,
                      x1_ref, h2_ref, lg_ref, *, d, alpha):
    shift1, scale1, gate1 = mod_ref[0, 0:1, :], mod_ref[0, 1:2, :], mod_ref[0, 2:3, :]
    shift2, scale2 = mod_ref[0, 3:4, :], mod_ref[0, 4:5, :]
    x = x_ref[...]
    hb = (x * (1.0 + scale1) + shift1).astype(BF16)
    ga = jnp.dot(hb, wg_ref[:, :d], preferred_element_type=F32)
    gb = jnp.dot(hb, wg_ref[:, d:], preferred_element_type=F32)
    pa = jnp.dot(oa_ref[...], wbr_ref[:d, :], preferred_element_type=F32)
    pb = jnp.dot(ob_ref[...], wbr_ref[d:, :], preferred_element_type=F32)
    merged = jax.nn.sigmoid(ga) * pa + jax.nn.sigmoid(gb) * pb
    y = jnp.dot(merged.astype(BF16), wo_ref[...], preferred_element_type=F32)
    x1 = _layer_norm(alpha * x + (1.0 + gate1) * y, ln_ref[0:1, :], ln_ref[1:2, :])
    x1_ref[...] = x1
    h2 = x1 * (1.0 + scale2) + shift2
    h2b = h2.astype(BF16)
    h2_ref[...] = h2b
    lg_ref[...] = _dot_split(h2, h2b, wr_ref) + br_ref[...]


def mixer_out(oa, ob, x2, mod, wg, wbr, wo, ln, wr, br, *, seq, alpha):
    n, d = x2.shape
    tpb = seq // TM
    row = lambda i: (i, 0)
    const = lambda shape: pl.BlockSpec(shape, lambda i: (0, 0), pipeline_mode=pl.Buffered(1))
    return pl.pallas_call(
        functools.partial(_mixer_out_kernel, d=d, alpha=alpha),
        grid=(n // TM,),
        in_specs=[pl.BlockSpec((TM, d), row), pl.BlockSpec((TM, d), row), pl.BlockSpec((TM, d), row),
                  pl.BlockSpec((1, 6, d), lambda i: (i // tpb, 0, 0)),
                  const((d, 2 * d)), const((2 * d, d)), const((d, d)), const((2, d)),
                  pl.BlockSpec((2, d, LANES), lambda i: (0, 0, 0), pipeline_mode=pl.Buffered(1)),
                  const((1, LANES))],
        out_specs=[pl.BlockSpec((TM, d), row), pl.BlockSpec((TM, d), row), pl.BlockSpec((TM, LANES), row)],
        out_shape=[jax.ShapeDtypeStruct((n, d), F32), jax.ShapeDtypeStruct((n, d), BF16),
                   jax.ShapeDtypeStruct((n, LANES), F32)],
        compiler_params=_cparams(("parallel",)),
        name="mixer_out",
    )(oa, ob, x2, mod, wg, wbr, wo, ln, wr, br)


def _tile_rows(t):
    rows = t * TOP_K + N_EXPERTS * (CHUNK - 1)
    return -(-rows // (8 * CHUNK)) * (8 * CHUNK)


def _dispatch_kernel(lg_ref, h2_ref, xs_ref, ys0_ref, rw_ref, cnt_ref, *, t, rows, nt):
    ys0_ref[...] = jnp.zeros_like(ys0_ref)

    @pl.when(pl.program_id(0) < nt)
    def _():
        _route_tile(lg_ref, h2_ref, xs_ref, rw_ref, cnt_ref, t=t, rows=rows)


def _route_tile(lg_ref, h2_ref, xs_ref, rw_ref, cnt_ref, *, t, rows):
    lane = lax.broadcasted_iota(jnp.int32, (t, LANES), 1).astype(F32)
    lg = lg_ref[...]
    onehots, vals = [], []
    for _ in range(TOP_K):
        mx = jnp.max(lg, axis=-1, keepdims=True)
        idx = jnp.min(jnp.where(lg == mx, lane, float(LANES)), axis=-1, keepdims=True)
        oh = lane == idx
        onehots.append(oh)
        vals.append(mx)
        lg = jnp.where(oh, -jnp.inf, lg)
    es = [jnp.exp(v - vals[0]) for v in vals]
    den = functools.reduce(lambda a, b: a + b, es)
    ws = [e / den for e in es]

    member = functools.reduce(lambda a, b: a + b, [jnp.where(oh, 1.0, 0.0) for oh in onehots])
    rr = lax.broadcasted_iota(jnp.int32, (t, t), 0)
    cc = lax.broadcasted_iota(jnp.int32, (t, t), 1)
    strict_lower = jnp.where(cc < rr, 1.0, 0.0).astype(BF16)
    rank = jnp.dot(strict_lower, member.astype(BF16), preferred_element_type=F32)
    cnt = jnp.sum(member, axis=0, keepdims=True)
    nch = jnp.floor((cnt + (CHUNK - 1)) * (1.0 / CHUNK))
    ur = lax.broadcasted_iota(jnp.int32, (LANES, LANES), 0)
    uc = lax.broadcasted_iota(jnp.int32, (LANES, LANES), 1)
    strict_upper = jnp.where(ur < uc, 1.0, 0.0).astype(BF16)
    toff = jnp.dot(jnp.broadcast_to(nch, (8, LANES)).astype(BF16), strict_upper,
                   preferred_element_type=F32)[0:1, :]
    pos = toff * float(CHUNK) + rank

    rw = jnp.full((t, LANES), -1.0, F32)
    for kk in range(TOP_K):
        r_k = jnp.sum(jnp.where(onehots[kk], pos, 0.0), axis=-1, keepdims=True)
        rw = jnp.where(lane == kk, r_k, rw)
        rw = jnp.where(lane == TOP_K + kk, ws[kk], rw)
    rw_ref[...] = rw
    cnt_ref[0] = jnp.broadcast_to(cnt, (8, LANES))

    rt = rw.T
    h2 = h2_ref[...]
    for r0 in range(0, rows, PERM_STRIP):
        riota = (lax.broadcasted_iota(jnp.int32, (PERM_STRIP, t), 0) + r0).astype(F32)
        pm = jnp.zeros((PERM_STRIP, t), F32)
        for kk in range(TOP_K):
            pm = jnp.where(riota == rt[kk:kk + 1, :], 1.0, pm)
        xs_ref[r0:r0 + PERM_STRIP, :] = jnp.dot(pm.astype(BF16), h2, preferred_element_type=F32).astype(BF16)


def dispatch(logits, h2):
    n, d = h2.shape
    t = min(ROUTE_T, n)
    rows = _tile_rows(t)
    nt = n // t
    assert 2 * BLOCK_CHUNKS * CHUNK <= rows
    tile = lambda i: (jnp.minimum(i, nt - 1), 0)
    return pl.pallas_call(
        functools.partial(_dispatch_kernel, t=t, rows=rows, nt=nt),
        grid=(nt + 1,),
        in_specs=[pl.BlockSpec((t, LANES), tile), pl.BlockSpec((t, d), tile)],
        out_specs=[pl.BlockSpec((rows, d), tile), pl.BlockSpec((rows, d), lambda i: (i, 0)),
                   pl.BlockSpec((t, LANES), tile),
                   pl.BlockSpec((1, 8, LANES), lambda i: (jnp.minimum(i, nt - 1), 0, 0))],
        out_shape=[jax.ShapeDtypeStruct((nt * rows, d), BF16),
                   jax.ShapeDtypeStruct(((nt + 1) * rows, d), BF16),
                   jax.ShapeDtypeStruct((n, LANES), F32), jax.ShapeDtypeStruct((nt, 8, LANES), F32)],
        compiler_params=_cparams(("arbitrary",)),
        name="dispatch",
    )(logits, h2)


def _num_blocks(nt, t):
    max_chunks = nt * ((t * TOP_K + N_EXPERTS * (CHUNK - 1)) // CHUNK)
    return -(-max_chunks // BLOCK_CHUNKS) + N_EXPERTS


def _routing_tables(cnt, t):
    nt = cnt.shape[0]
    e = N_EXPERTS
    cpt = _tile_rows(t) // CHUNK
    nb = _num_blocks(nt, t)
    nch = (cnt + (CHUNK - 1)) // CHUNK
    toff = jnp.cumsum(nch, axis=1) - nch
    n_e = jnp.sum(nch, axis=0)
    nblk = (n_e + (BLOCK_CHUNKS - 1)) // BLOCK_CHUNKS
    bend = jnp.cumsum(nblk)
    bstart = bend - nblk
    total_blocks = bend[-1]
    cumj_incl = jnp.cumsum(nch, axis=0)
    cumj = cumj_incl - nch

    b = jnp.arange(nb, dtype=jnp.int32)
    last = jnp.maximum(total_blocks - 1, 0)
    bex = jnp.minimum(jnp.sum(bend[None, :] <= jnp.minimum(b, last)[:, None], axis=1), e - 1).astype(jnp.int32)
    bvalid = b < total_blocks
    sel = (bex[:, None] == jnp.arange(e, dtype=jnp.int32)[None, :]).astype(jnp.int32)
    row = lambda tab: jnp.sum(sel[:, :, None] * tab.T[None, :, :], axis=1)
    cum_b, excl_b, toff_b = row(cumj_incl), row(cumj), row(toff)
    n_b = jnp.sum(sel * n_e[None, :], axis=1)
    s0 = (b - jnp.sum(sel * bstart[None, :], axis=1)) * BLOCK_CHUNKS
    s = s0[:, None] + jnp.arange(BLOCK_CHUNKS, dtype=jnp.int32)[None, :]
    valid = bvalid[:, None] & (s < n_b[:, None])
    j = jnp.minimum(jnp.sum(cum_b[:, None, :] <= s[:, :, None], axis=2), nt - 1)
    jsel = (j[:, :, None] == jnp.arange(nt, dtype=jnp.int32)[None, None, :]).astype(jnp.int32)
    pick = lambda tab_b: jnp.sum(jsel * tab_b[:, None, :], axis=2)
    chunk = j * cpt + pick(toff_b) + (s - pick(excl_b))
    src = jnp.where(valid, chunk, 0).astype(jnp.int32).reshape(-1)
    spare = nt * cpt + (b[:, None] % 2) * BLOCK_CHUNKS + jnp.arange(BLOCK_CHUNKS, dtype=jnp.int32)[None, :]
    dst = jnp.where(valid, chunk, spare).astype(jnp.int32).reshape(-1)
    return bex, bvalid.astype(jnp.int32), src, dst, nb


def _expert_kernel(bex_ref, bvalid_ref, src_ref, dst_ref, xs_ref, wg_ref, bg_ref, wu_ref, bu_ref, wd_ref, bd_ref,
                   ys_in_ref, ys_ref, xbuf, ybuf, wbf, sem_in, sem, *, nb):
    del ys_in_ref
    i = pl.program_id(0)
    slot = i % 2

    def in_copy(step, sl, c):
        row = pl.multiple_of(src_ref[step * BLOCK_CHUNKS + c] * CHUNK, CHUNK)
        return pltpu.make_async_copy(xs_ref.at[pl.ds(row, CHUNK), :],
                                     xbuf.at[sl, pl.ds(c * CHUNK, CHUNK), :], sem_in.at[sl])

    def out_copy(step, sl, c):
        row = pl.multiple_of(dst_ref[step * BLOCK_CHUNKS + c] * CHUNK, CHUNK)
        return pltpu.make_async_copy(ybuf.at[sl, pl.ds(c * CHUNK, CHUNK), :],
                                     ys_ref.at[pl.ds(row, CHUNK), :], sem.at[sl])

    def gather(step, sl):
        @pl.when(bvalid_ref[step] == 1)
        def _():
            for c in range(BLOCK_CHUNKS):
                in_copy(step, sl, c).start()

    @pl.when(i == 0)
    def _():
        gather(0, 0)

    @pl.when(i + 1 < nb)
    def _():
        gather(jnp.minimum(i + 1, nb - 1), 1 - slot)

    def wait_step(step, sl):
        @pl.when(bvalid_ref[step] == 1)
        def _():
            for c in range(BLOCK_CHUNKS):
                out_copy(step, sl, c).wait()

    @pl.when(i >= 2)
    def _():
        wait_step(jnp.maximum(i - 2, 0), slot)

    prev = jnp.maximum(i - 1, 0)
    new_expert = (i == 0) | (bex_ref[i] != bex_ref[prev])

    @pl.when((bvalid_ref[i] == 1) & new_expert)
    def _():
        wbf[0] = wg_ref[0, 0].astype(BF16)
        wbf[1] = wu_ref[0, 0].astype(BF16)
        wbf[2] = wd_ref[0, 0].astype(BF16)

    @pl.when(bvalid_ref[i] == 1)
    def _():
        for c in range(BLOCK_CHUNKS):
            in_copy(i, slot, c).wait()
        xb = xbuf[slot]
        g = jnp.minimum(jnp.dot(xb, wbf[0], preferred_element_type=F32) + bg_ref[0, 0], SWIGLU_LIMIT)
        u = jnp.clip(jnp.dot(xb, wbf[1], preferred_element_type=F32) + bu_ref[0, 0], -SWIGLU_LIMIT, SWIGLU_LIMIT)
        a = (u + 1.0) * (g * jax.nn.sigmoid(SWIGLU_ALPHA * g))
        y = jnp.dot(a.astype(BF16), wbf[2], preferred_element_type=F32) + bd_ref[0, 0]
        ybuf[slot] = y.astype(BF16)
        for c in range(BLOCK_CHUNKS):
            out_copy(i, slot, c).start(priority=c % 2)

    @pl.when(i == nb - 1)
    def _():
        @pl.when(i >= 1)
        def _():
            wait_step(jnp.maximum(i - 1, 0), 1 - slot)
        wait_step(i, slot)


def experts(xs, ys_init, bex, bvalid, src, dst, w_gate, b_gate, w_up, b_up, w_down, b_down, *, nb, layer):
    d = xs.shape[1]
    depth, e, _, de = w_gate.shape
    rows = BLOCK_CHUNKS * CHUNK

    wspec = lambda shape: pl.BlockSpec((1,) + shape, lambda i, be, bv, sc, dc: (layer, be[i], 0, 0))
    hbm = pl.BlockSpec(memory_space=pl.ANY)
    grid_spec = pltpu.PrefetchScalarGridSpec(
        num_scalar_prefetch=4,
        grid=(nb,),
        in_specs=[hbm, wspec((1, d, de)), wspec((1, 1, de)), wspec((1, d, de)), wspec((1, 1, de)),
                  wspec((1, de, d)), wspec((1, 1, d)), hbm],
        out_specs=hbm,
        scratch_shapes=[pltpu.VMEM((2, rows, d), BF16), pltpu.VMEM((2, rows, d), BF16),
                        pltpu.VMEM((3, d, de), BF16), pltpu.SemaphoreType.DMA((2,)),
                        pltpu.SemaphoreType.DMA((2,))],
    )
    n_in = 4 + 1 + 6
    return pl.pallas_call(
        functools.partial(_expert_kernel, nb=nb),
        grid_spec=grid_spec,
        out_shape=jax.ShapeDtypeStruct(ys_init.shape, BF16),
        input_output_aliases={n_in: 0},
        compiler_params=_cparams(("arbitrary",)),
        name="experts",
    )(bex, bvalid, src, dst, xs,
      w_gate, b_gate.reshape(depth, e, 1, de), w_up, b_up.reshape(depth, e, 1, de),
      w_down, b_down.reshape(depth, e, 1, d), ys_init)


def _combine_kernel(ys_ref, rw_ref, x1_ref, mod_ref, ln_ref, o_ref, *, t, rows, alpha):
    gate2 = mod_ref[0, 5:6, :]
    rw = rw_ref[...]
    y2 = jnp.zeros((t, x1_ref.shape[1]), F32)
    for c0 in range(0, rows, PERM_STRIP):
        liota = (lax.broadcasted_iota(jnp.int32, (t, PERM_STRIP), 1) + c0).astype(F32)
        wm = jnp.zeros((t, PERM_STRIP), F32)
        for kk in range(TOP_K):
            wm = jnp.where(liota == rw[:, kk:kk + 1], rw[:, TOP_K + kk:TOP_K + kk + 1], wm)
        y2 = y2 + jnp.dot(wm.astype(BF16), ys_ref[c0:c0 + PERM_STRIP, :], preferred_element_type=F32)
    o_ref[...] = _layer_norm(alpha * x1_ref[...] + (1.0 + gate2) * y2, ln_ref[0:1, :], ln_ref[1:2, :])


def combine(ys, rw, x1, mod, ln, *, seq, alpha):
    n, d = x1.shape
    t = min(ROUTE_T, n)
    rows = _tile_rows(t)
    tpb = max(seq // t, 1)
    return pl.pallas_call(
        functools.partial(_combine_kernel, t=t, rows=rows, alpha=alpha),
        grid=(n // t,),
        in_specs=[pl.BlockSpec((rows, d), lambda i: (i, 0)),
                  pl.BlockSpec((t, LANES), lambda i: (i, 0)),
                  pl.BlockSpec((t, d), lambda i: (i, 0)),
                  pl.BlockSpec((1, 6, d), lambda i: (i // tpb, 0, 0)),
                  pl.BlockSpec((2, d), lambda i: (0, 0))],
        out_specs=pl.BlockSpec((t, d), lambda i: (i, 0)),
        out_shape=jax.ShapeDtypeStruct((n, d), F32),
        compiler_params=_cparams(("parallel",)),
        name="combine",
    )(ys, rw, x1, mod, ln)


def moe_grouped(logits, h2, w_gate, b_gate, w_up, b_up, w_down, b_down, *, layer):
    n, d = h2.shape
    t = min(ROUTE_T, n)
    xs, ys_init, rw, cnt = dispatch(logits, h2)
    cnt_i = cnt[:, 0, :N_EXPERTS].astype(jnp.int32)
    bex, bvalid, src, dst, nb = _routing_tables(cnt_i, t)
    ys = experts(xs, ys_init, bex, bvalid, src, dst, w_gate, b_gate, w_up, b_up, w_down, b_down,
                 nb=nb, layer=layer)
    return ys, rw


def kernel(x, c, w_in, fox_f_bias, hg_lb_logits, hg_norm_w, w_branch, w_out, ada_w, ada_b,
           ln1_g, ln1_b, w_router, b_router, w_gate, b_gate, w_up, b_up, w_down, b_down,
           ln2_g, ln2_b):
    batch, seq, d = x.shape
    depth = w_in.shape[0]
    n = batch * seq
    alpha = (2 * depth) ** 0.25
    hgw = HG_HEADS * HG_DK
    foxw = FOX_HEADS * FOX_DH
    off_af = 4 * hgw + 3 * foxw
    off_g = off_af + FOX_HEADS

    mod_all = ada_mod(c, ada_w, ada_b).reshape(depth, batch, 6, d)
    x2 = x.reshape(n, d)

    for l in range(depth):
        mod = mod_all[l]
        w7 = w_in[l, :, :off_af].astype(BF16)
        waf = _split_hi_lo(jnp.pad(w_in[l, :, off_af:off_g], ((0, 0), (0, LANES - FOX_HEADS))))
        wg = w_in[l, :, off_g:].astype(BF16)
        hq, hf, hi, hg, aq, ak, av, af, qq, kk, qk = in_proj(x2, mod, w7, waf, seq)

        negf, stats = fox_decay(af, fox_f_bias[l], qq, kk, qk, batch=batch, seq=seq, heads=FOX_HEADS)
        first = fox_first_blocks(stats, heads=FOX_HEADS, sub=min(FOX_T, seq) // min(FOX_TK, seq))
        o_a = hgrn2(hq, hf, hi, hg, hg_lb_logits, hg_norm_w[l], layer=l, batch=batch, seq=seq)
        o_b = fox_attn(first, aq, ak, av, negf, batch=batch, seq=seq)

        wr = _split_hi_lo(jnp.pad(w_router[l], ((0, 0), (0, LANES - N_EXPERTS))))
        br = jnp.pad(b_router[l], (0, LANES - N_EXPERTS), constant_values=NEG_BIG).reshape(1, LANES)
        x1, h2, logits = mixer_out(
            o_a, o_b, x2, mod, wg, w_branch[l].astype(BF16), w_out[l].astype(BF16),
            jnp.stack([ln1_g[l], ln1_b[l]]), wr, br, seq=seq, alpha=alpha)

        ys, rw = moe_grouped(logits, h2, w_gate, b_gate, w_up, b_up, w_down, b_down, layer=l)
        x2 = combine(ys, rw, x1, mod, jnp.stack([ln2_g[l], ln2_b[l]]), seq=seq, alpha=alpha)
    return x2.reshape(batch, seq, d)
```
